```python
import math
import jax, jax.numpy as jnp
from jax import lax
import numpy as np

D_MODEL = 1024
BATCH = 4
SEQ = 8192
DEPTH = 1

HEAD_DIM = 64
SB_HEADS = 8
MB_HEADS = 8
SB_WIDTH = SB_HEADS * HEAD_DIM
MB_WIDTH = MB_HEADS * HEAD_DIM
SB_QBLOCK = 128
MB_BLOCK = 256
MB_TOPK = 3
MB_QCHUNK = 64
REL_BUCKETS = 32
REL_MAX_DIST = 128
IN_WIDTH = 3 * SB_WIDTH + 3 * MB_WIDTH + 2 * D_MODEL
N_EXPERTS = 64
N_GROUPS = 8
TOPK_GROUPS = 4
TOPK_EXPERTS = 8
EXPERT_FF = 256
SHARED_FF = 256
ROUTED_SCALE = 2.5
MOE_ROW_BLOCK = 256
N_MOD = 6
EPS = 1e-6
NEG = -1e30

kernel_name = "hybrid_stickbreak_moba_moe_adaln"


def rms_norm(x, g):
    x32 = x.astype(jnp.float32)
    y = x32 * lax.rsqrt(jnp.mean(x32 * x32, axis=-1, keepdims=True) + EPS)
    return (y * g.astype(jnp.float32)).astype(x.dtype)


def modulate(h, shift, scale):
    return h * (1.0 + scale[:, None, :]) + shift[:, None, :]


def split_heads(t, n_heads):
    b, s, _ = t.shape
    return t.reshape(b, s, n_heads, HEAD_DIM).transpose(0, 2, 1, 3)


def merge_heads(t):
    b, h, s, d = t.shape
    return t.transpose(0, 2, 1, 3).reshape(b, s, h * d)


def t5_bucket(dist):
    n = jnp.maximum(dist, 0)
    max_exact = REL_BUCKETS // 2
    nf = jnp.maximum(n, 1).astype(jnp.float32)
    large = max_exact + (jnp.log(nf / max_exact) / math.log(REL_MAX_DIST / max_exact)
                         * (REL_BUCKETS - max_exact)).astype(jnp.int32)
    large = jnp.minimum(large, REL_BUCKETS - 1)
    return jnp.where(n < max_exact, n, large)


def stick_breaking_attention(q, k, v):
    s_len = q.shape[2]
    scale = HEAD_DIM ** -0.5
    outs = []
    for i in range(s_len // SB_QBLOCK):
        q0 = i * SB_QBLOCK
        kend = q0 + SB_QBLOCK
        qb = q[:, :, q0:kend]
        kb = k[:, :, :kend]
        vb = v[:, :, :kend]
        z = jnp.einsum('bhqd,bhkd->bhqk', qb, kb).astype(jnp.float32) * scale
        qpos = q0 + jnp.arange(SB_QBLOCK)
        kpos = jnp.arange(kend)
        past = kpos[None, :] < qpos[:, None]
        log_beta = jax.nn.log_sigmoid(z)
        log_rest = jnp.where(past, jax.nn.log_sigmoid(-z), 0.0)
        log_skip = lax.cumsum(log_rest, axis=3, reverse=True) - log_rest
        a = jnp.where(past, jnp.exp(log_beta + log_skip), 0.0)
        outs.append(jnp.einsum('bhqk,bhkd->bhqd', a.astype(v.dtype), vb))
    return jnp.concatenate(outs, axis=2)


def moba_attention(q, k, v, rel_bias):
    b, h, s_len, dh = q.shape
    nb = -(-s_len // MB_BLOCK)
    sp = nb * MB_BLOCK
    pad = ((0, 0), (0, 0), (0, sp - s_len), (0, 0))
    kp = jnp.pad(k, pad)
    vp = jnp.pad(v, pad)
    kblk = kp.reshape(b, h, nb, MB_BLOCK, dh)
    vblk = vp.reshape(b, h, nb, MB_BLOCK, dh)
    kbar = jnp.mean(kblk.astype(jnp.float32), axis=3)
    topk = min(MB_TOPK, nb)
    scale = dh ** -0.5
    bi = jnp.arange(b)[:, None, None, None]
    hi = jnp.arange(h)[None, :, None, None]
    hi5 = hi[..., None]
    offs = jnp.arange(MB_BLOCK)

    def chunk(ci):
        q0 = ci * MB_QCHUNK
        qc = lax.dynamic_slice_in_dim(q, q0, MB_QCHUNK, axis=2)
        qpos = q0 + jnp.arange(MB_QCHUNK)
        blk = q0 // MB_BLOCK
        gate = jnp.einsum('bhqd,bhnd->bhqn', qc.astype(jnp.float32), kbar)
        gate = jnp.where(jnp.arange(nb) < blk, gate, NEG)
        _, sel = lax.top_k(gate, topk)
        valid = sel < blk
        ksel = kblk[bi, hi, sel]
        vsel = vblk[bi, hi, sel]
        kpos_sel = sel[..., None] * MB_BLOCK + offs
        bias_sel = rel_bias[hi5, t5_bucket(qpos[:, None, None] - kpos_sel)]
        s_sel = jnp.einsum('bhqd,bhqjnd->bhqjn', qc, ksel).astype(jnp.float32) * scale + bias_sel
        s_sel = jnp.where(valid[..., None], s_sel, NEG).reshape(b, h, MB_QCHUNK, topk * MB_BLOCK)
        k_own = lax.dynamic_slice_in_dim(kp, blk * MB_BLOCK, MB_BLOCK, axis=2)
        v_own = lax.dynamic_slice_in_dim(vp, blk * MB_BLOCK, MB_BLOCK, axis=2)
        dist = qpos[:, None] - (blk * MB_BLOCK + offs)[None, :]
        s_own = jnp.einsum('bhqd,bhnd->bhqn', qc, k_own).astype(jnp.float32) * scale
        s_own = jnp.where(dist >= 0, s_own + rel_bias[:, t5_bucket(dist)], NEG)
        p = jax.nn.softmax(jnp.concatenate([s_sel, s_own], axis=-1), axis=-1)
        p_sel = p[..., :topk * MB_BLOCK].reshape(b, h, MB_QCHUNK, topk, MB_BLOCK)
        p_own = p[..., topk * MB_BLOCK:]
        return (jnp.einsum('bhqjn,bhqjnd->bhqd', p_sel.astype(v.dtype), vsel)
                + jnp.einsum('bhqn,bhnd->bhqd', p_own.astype(v.dtype), v_own))

    outs = lax.map(chunk, jnp.arange(s_len // MB_QCHUNK))
    return outs.transpose(1, 2, 0, 3, 4).reshape(b, h, s_len, dh)


def swiglu(x, w_gate, w_up, w_down):
    return (jax.nn.silu(x @ w_gate) * (x @ w_up)) @ w_down


def moe_ffn(xt, w_router, router_bias, w_gate_e, w_up_e, w_down_e, w_gate_sh, w_up_sh, w_down_sh):
    t = xt.shape[0]
    scores = jax.nn.sigmoid((xt @ w_router).astype(jnp.float32))
    choice = scores + router_bias.astype(jnp.float32)
    grp = choice.reshape(t, N_GROUPS, N_EXPERTS // N_GROUPS)
    grp_score = jnp.sum(lax.top_k(grp, 2)[0], axis=-1)
    _, gidx = lax.top_k(grp_score, TOPK_GROUPS)
    gmask = jnp.any(gidx[..., None] == jnp.arange(N_GROUPS), axis=1)
    emask = jnp.repeat(gmask, N_EXPERTS // N_GROUPS, axis=1)
    _, eidx = lax.top_k(jnp.where(emask, choice, NEG), TOPK_EXPERTS)
    wts = jnp.take_along_axis(scores, eidx, axis=1)
    wts = wts / jnp.sum(wts, axis=-1, keepdims=True) * ROUTED_SCALE

    n_assign = t * TOPK_EXPERTS
    e_flat = eidx.reshape(n_assign)
    tok_flat = jnp.repeat(jnp.arange(t, dtype=jnp.int32), TOPK_EXPERTS)
    w_flat = wts.reshape(n_assign)
    order = jnp.argsort(e_flat)
    e_s = e_flat[order]
    counts = jnp.bincount(e_flat, length=N_EXPERTS)
    start = jnp.cumsum(counts) - counts
    pcounts = (counts + MOE_ROW_BLOCK - 1) // MOE_ROW_BLOCK * MOE_ROW_BLOCK
    pend = jnp.cumsum(pcounts)
    pstart = pend - pcounts
    dest = pstart[e_s] + (jnp.arange(n_assign) - start[e_s])
    n_blk = -(-n_assign // MOE_ROW_BLOCK) + N_EXPERTS
    n_rows = n_blk * MOE_ROW_BLOCK
    tok_buf = jnp.zeros((n_rows,), jnp.int32).at[dest].set(tok_flat[order])
    w_buf = jnp.zeros((n_rows,), jnp.float32).at[dest].set(w_flat[order])
    blk_expert = jnp.minimum(
        jnp.searchsorted(pend, jnp.arange(n_blk) * MOE_ROW_BLOCK, side='right'), N_EXPERTS - 1)

    def body(acc, bidx):
        rows = lax.dynamic_slice_in_dim(tok_buf, bidx * MOE_ROW_BLOCK, MOE_ROW_BLOCK)
        wr = lax.dynamic_slice_in_dim(w_buf, bidx * MOE_ROW_BLOCK, MOE_ROW_BLOCK)
        e = blk_expert[bidx]
        ye = swiglu(xt[rows], w_gate_e[e], w_up_e[e], w_down_e[e]) * wr[:, None]
        return acc.at[rows].add(ye.astype(acc.dtype)), None

    routed, _ = lax.scan(body, jnp.zeros_like(xt), jnp.arange(n_blk))
    return routed + swiglu(xt, w_gate_sh, w_up_sh, w_down_sh)


def setup_inputs(seed: int = 0) -> dict:
    key = jax.random.key(seed)
    ks = jax.random.split(key, 24)
    f32 = jnp.float32
    L, D, E = DEPTH, D_MODEL, N_EXPERTS

    def nrm(k, shape, s):
        return jax.random.normal(k, shape, f32) * s

    return {
        "x": nrm(ks[0], (BATCH, SEQ, D), 1.0),
        "c": nrm(ks[1], (BATCH, D), 1.0),
        "norm1_g": 1.0 + nrm(ks[2], (L, D), 0.05),
        "norm2_g": 1.0 + nrm(ks[3], (L, D), 0.05),
        "w_ada": nrm(ks[4], (L, D, N_MOD * D), 0.5 * D ** -0.5),
        "b_ada": nrm(ks[5], (L, N_MOD * D), 0.02),
        "w_in": nrm(ks[6], (L, D, IN_WIDTH), D ** -0.5),
        "w_branch_sb": nrm(ks[7], (L, SB_WIDTH, D), SB_WIDTH ** -0.5),
        "w_branch_mb": nrm(ks[8], (L, MB_WIDTH, D), MB_WIDTH ** -0.5),
        "w_out": nrm(ks[9], (L, D, D), D ** -0.5),
        "w_router": nrm(ks[10], (L, D, E), D ** -0.5),
        "router_bias": nrm(ks[11], (L, E), 0.01),
        "w_gate_e": nrm(ks[12], (L, E, D, EXPERT_FF), D ** -0.5),
        "w_up_e": nrm(ks[13], (L, E, D, EXPERT_FF), D ** -0.5),
        "w_down_e": nrm(ks[14], (L, E, EXPERT_FF, D), EXPERT_FF ** -0.5),
        "w_gate_sh": nrm(ks[15], (L, D, SHARED_FF), D ** -0.5),
        "w_up_sh": nrm(ks[16], (L, D, SHARED_FF), D ** -0.5),
        "w_down_sh": nrm(ks[17], (L, SHARED_FF, D), SHARED_FF ** -0.5),
        "rel_bias": nrm(ks[18], (MB_HEADS, REL_BUCKETS), 0.5),
        "final_g": 1.0 + nrm(ks[19], (D,), 0.05),
    }


def reference(x, c, norm1_g, norm2_g, w_ada, b_ada, w_in, w_branch_sb, w_branch_mb, w_out,
              w_router, router_bias, w_gate_e, w_up_e, w_down_e, w_gate_sh, w_up_sh, w_down_sh,
              rel_bias, final_g):
    b, s, d = x.shape
    cuts = [SB_WIDTH, 2 * SB_WIDTH, 3 * SB_WIDTH,
            3 * SB_WIDTH + MB_WIDTH, 3 * SB_WIDTH + 2 * MB_WIDTH, 3 * SB_WIDTH + 3 * MB_WIDTH,
            3 * SB_WIDTH + 3 * MB_WIDTH + D_MODEL]
    for l in range(DEPTH):
        mod = jax.nn.silu(c) @ w_ada[l] + b_ada[l]
        shift1, scale1, gate1, shift2, scale2, gate2 = jnp.split(mod, N_MOD, axis=-1)

        h = modulate(rms_norm(x, norm1_g[l]), shift1, scale1)
        proj = h @ w_in[l]
        q_sb, k_sb, v_sb, q_mb, k_mb, v_mb, g_sb, g_mb = jnp.split(proj, cuts, axis=-1)
        o_sb = stick_breaking_attention(split_heads(q_sb, SB_HEADS), split_heads(k_sb, SB_HEADS),
                                        split_heads(v_sb, SB_HEADS))
        o_mb = moba_attention(split_heads(q_mb, MB_HEADS), split_heads(k_mb, MB_HEADS),
                              split_heads(v_mb, MB_HEADS), rel_bias)
        merged = (jax.nn.sigmoid(g_sb) * (merge_heads(o_sb) @ w_branch_sb[l])
                  + jax.nn.sigmoid(g_mb) * (merge_heads(o_mb) @ w_branch_mb[l]))
        x = x + gate1[:, None, :] * (merged @ w_out[l])

        h2 = modulate(rms_norm(x, norm2_g[l]), shift2, scale2)
        y = moe_ffn(h2.reshape(b * s, d), w_router[l], router_bias[l], w_gate_e[l], w_up_e[l],
                    w_down_e[l], w_gate_sh[l], w_up_sh[l], w_down_sh[l]).reshape(b, s, d)
        x = x + gate2[:, None, :] * y
    return rms_norm(x, final_g)
```

```python
import functools

import jax
import jax.numpy as jnp
from jax import lax
from jax.experimental import pallas as pl
from jax.experimental.pallas import tpu as pltpu

F32 = jnp.float32
BF16 = jnp.bfloat16
I32 = jnp.int32

HEAD_DIM = 64
SB_HEADS = 8
MB_HEADS = 8
SB_WIDTH = SB_HEADS * HEAD_DIM
MB_WIDTH = MB_HEADS * HEAD_DIM
MB_BLOCK = 256
MB_TOPK = 3
REL_BUCKETS = 32
REL_MAX_DIST = 128
N_EXPERTS = 64
N_GROUPS = 8
GROUP_SIZE = N_EXPERTS // N_GROUPS
TOPK_GROUPS = 4
TOPK_EXPERTS = 8
ROUTED_SCALE = 2.5
MOE_ROW_BLOCK = 256
N_MOD = 6
EPS = 1e-6
NEG = -1e30

LANES = 128
HEAD_PAIR = LANES // HEAD_DIM
ATT_TILE = 256
VMEM_LIMIT = 56 * 1024 * 1024


def _cparams(n_axes):
    return pltpu.CompilerParams(dimension_semantics=("arbitrary",) * n_axes,
                                vmem_limit_bytes=VMEM_LIMIT)


def _dot(a, b):
    return jnp.dot(a, b, preferred_element_type=F32)


def _dot_nt(a, b):
    return lax.dot_general(a, b, (((1,), (1,)), ((), ())), preferred_element_type=F32)


def _split3(a):
    p0 = a.astype(BF16)
    r0 = a - p0.astype(F32)
    p1 = r0.astype(BF16)
    p2 = (r0 - p1.astype(F32)).astype(BF16)
    return p0, p1, p2


def _sigmoid(x):
    return 1.0 / (1.0 + jnp.exp(-x))


def _rms_mod(x, g, shift, scale):
    y = x * lax.rsqrt(jnp.mean(x * x, axis=-1, keepdims=True) + EPS) * g
    return y * (1.0 + scale) + shift


def _ada_kernel(c_ref, w_ref, b_ref, o_ref):
    c = c_ref[...]
    s = c * _sigmoid(c)
    s0, s1, _ = _split3(s)
    w0, w1, _ = _split3(w_ref[...])
    o_ref[...] = _dot(s0, w0) + _dot(s0, w1) + _dot(s1, w0) + b_ref[...]


def _ada(c, w_ada, b_ada):
    b, d = c.shape
    n = w_ada.shape[1]
    tn = 1536
    return pl.pallas_call(
        _ada_kernel,
        out_shape=jax.ShapeDtypeStruct((b, n), F32),
        grid=(n // tn,),
        in_specs=[pl.BlockSpec((b, d), lambda j: (0, 0)),
                  pl.BlockSpec((d, tn), lambda j: (0, j)),
                  pl.BlockSpec((1, tn), lambda j: (0, j))],
        out_specs=pl.BlockSpec((b, tn), lambda j: (0, j)),
        compiler_params=_cparams(1),
        name="ada",
    )(c, w_ada, b_ada.reshape(1, n))


def _in_proj_kernel(x_ref, mod_ref, g_ref, wqkv_ref, wg_ref, qkv_ref, gate_ref):
    h = _rms_mod(x_ref[0], g_ref[...], mod_ref[0, 0:1, :], mod_ref[0, 1:2, :]).astype(BF16)
    qkv_ref[0] = _dot(h, wqkv_ref[...]).astype(BF16)
    gate_ref[0] = _sigmoid(_dot(h, wg_ref[...])).astype(BF16)


def _in_proj(x, mod, g, wqkv, wg, tm=512):
    b, s, d = x.shape
    nq, ng = wqkv.shape[1], wg.shape[1]
    return pl.pallas_call(
        _in_proj_kernel,
        out_shape=(jax.ShapeDtypeStruct((b, s, nq), BF16), jax.ShapeDtypeStruct((b, s, ng), BF16)),
        grid=(b, s // tm),
        in_specs=[pl.BlockSpec((1, tm, d), lambda bi, i: (bi, i, 0)),
                  pl.BlockSpec((1, N_MOD, d), lambda bi, i: (bi, 0, 0)),
                  pl.BlockSpec((1, d), lambda bi, i: (0, 0)),
                  pl.BlockSpec((d, nq), lambda bi, i: (0, 0)),
                  pl.BlockSpec((d, ng), lambda bi, i: (0, 0))],
        out_specs=(pl.BlockSpec((1, tm, nq), lambda bi, i: (bi, i, 0)),
                   pl.BlockSpec((1, tm, ng), lambda bi, i: (bi, i, 0))),
        compiler_params=_cparams(2),
        name="in_proj",
    )(x, mod, g, wqkv, wg)


def _head_mask(h):
    lane = lax.broadcasted_iota(I32, (ATT_TILE, LANES), 1)
    return (lane >= h * HEAD_DIM) & (lane < (h + 1) * HEAD_DIM)


def _sb_kernel(q_ref, k_ref, v_ref, o_ref):
    t = ATT_TILE
    i = pl.program_id(2)
    row = lax.broadcasted_iota(I32, (t, t), 0)
    col = lax.broadcasted_iota(I32, (t, t), 1)
    later = jnp.where(row > col, 1.0, 0.0).astype(BF16)
    past = col < row
    q = q_ref[0]
    outs = []
    for h in range(HEAD_PAIR):
        qh = jnp.where(_head_mask(h), q, jnp.zeros_like(q))

        def tile(j, carry, acc, diag, qh=qh):
            start = pl.multiple_of(j * t, t)
            kj = k_ref[0, pl.ds(start, t), :]
            vj = v_ref[0, pl.ds(start, t), :]
            z = _dot_nt(qh, kj)
            sp = jnp.maximum(z, 0.0) + jnp.log(1.0 + jnp.exp(-jnp.abs(z)))
            log_rest = -sp
            if diag:
                log_rest = jnp.where(past, log_rest, 0.0)
            hi = log_rest.astype(BF16)
            lo = (log_rest - hi.astype(F32)).astype(BF16)
            log_skip = _dot(hi, later) + _dot(lo, later) + carry
            a = jnp.exp(z - sp + log_skip)
            if diag:
                a = jnp.where(past, a, 0.0)
            acc = acc + _dot(a.astype(BF16), vj)
            carry = carry + jnp.sum(log_rest, axis=1, keepdims=True)
            return carry, acc

        carry, acc = tile(i, jnp.zeros((t, 1), F32), jnp.zeros((t, LANES), F32), True)

        def body(n, c, tile=tile):
            return tile(i - 1 - n, c[0], c[1], False)

        carry, acc = lax.fori_loop(0, i, body, (carry, acc))
        outs.append(acc)
    o_ref[0] = jnp.where(_head_mask(0), outs[0], outs[1]).astype(BF16)


def _sb_attn(qkv, q_col, k_col, v_col):
    b, s, _ = qkv.shape
    t = ATT_TILE
    npair = SB_HEADS // HEAD_PAIR
    return pl.pallas_call(
        _sb_kernel,
        out_shape=jax.ShapeDtypeStruct((b, s, SB_WIDTH), BF16),
        grid=(b, npair, s // t),
        in_specs=[pl.BlockSpec((1, t, LANES), lambda bi, p, i: (bi, i, q_col + p)),
                  pl.BlockSpec((1, s, LANES), lambda bi, p, i: (bi, 0, k_col + p)),
                  pl.BlockSpec((1, s, LANES), lambda bi, p, i: (bi, 0, v_col + p))],
        out_specs=pl.BlockSpec((1, t, LANES), lambda bi, p, i: (bi, i, p)),
        compiler_params=_cparams(3),
        name="sb_attn",
    )(qkv, qkv, qkv)


def _mb_kernel(q_ref, k_ref, v_ref, bias_ref, o_ref, kbar_ref, *, nb):
    t = ATT_TILE
    i = pl.program_id(2)

    @pl.when(i == 0)
    def _():
        kbar_ref[...] = jnp.zeros_like(kbar_ref)
        for n in range(nb):
            kbar_ref[n:n + 1, :] = jnp.mean(k_ref[0, n * t:(n + 1) * t, :].astype(F32), axis=0, keepdims=True)

    q = q_ref[0]
    kb0, kb1, kb2 = _split3(kbar_ref[...])
    blk = lax.broadcasted_iota(I32, (t, LANES), 1)
    start_i = pl.multiple_of(i * t, t)
    k_own = k_ref[0, pl.ds(start_i, t), :]
    v_own = v_ref[0, pl.ds(start_i, t), :]
    outs = []
    for h in range(HEAD_PAIR):
        qh = jnp.where(_head_mask(h), q, jnp.zeros_like(q))
        gate = _dot_nt(qh, kb0) + _dot_nt(qh, kb1) + _dot_nt(qh, kb2)
        gate = jnp.where(blk < i, gate, -jnp.inf)
        sel_bias = jnp.full((t, LANES), NEG, F32)
        for _ in range(MB_TOPK):
            m = jnp.max(gate, axis=1, keepdims=True)
            idx = jnp.min(jnp.where(gate == m, blk, LANES), axis=1, keepdims=True)
            pick = (blk == idx) & (m > -jnp.inf)
            sel_bias = jnp.where(pick, 0.0, sel_bias)
            gate = jnp.where(pick, -jnp.inf, gate)

        s_own = _dot_nt(qh, k_own) + bias_ref[h, 0]
        m_run = jnp.max(s_own, axis=1, keepdims=True)
        p = jnp.exp(s_own - m_run)
        l_run = jnp.sum(p, axis=1, keepdims=True)
        acc = _dot(p.astype(BF16), v_own)

        def body(j, c, qh=qh, sel_bias=sel_bias, h=h):
            m_run, l_run, acc = c
            start = pl.multiple_of(j * t, t)
            kj = k_ref[0, pl.ds(start, t), :]
            vj = v_ref[0, pl.ds(start, t), :]
            sel_col = jnp.sum(jnp.where(blk == j, sel_bias, 0.0), axis=1, keepdims=True)
            bias = bias_ref[h, jnp.where(j == i - 1, 1, 2)]
            s = _dot_nt(qh, kj) + bias + sel_col
            m_new = jnp.maximum(m_run, jnp.max(s, axis=1, keepdims=True))
            alpha = jnp.exp(m_run - m_new)
            p = jnp.exp(s - m_new)
            l_new = alpha * l_run + jnp.sum(p, axis=1, keepdims=True)
            acc = alpha * acc + _dot(p.astype(BF16), vj)
            return m_new, l_new, acc

        m_run, l_run, acc = lax.fori_loop(0, i, body, (m_run, l_run, acc))
        outs.append(acc / l_run)
    o_ref[0] = jnp.where(_head_mask(0), outs[0], outs[1]).astype(BF16)


def _mb_attn(qkv, bias_tab, q_col, k_col, v_col):
    b, s, _ = qkv.shape
    t = ATT_TILE
    nb = s // t
    assert t == MB_BLOCK and s % t == 0 and nb <= LANES
    npair = MB_HEADS // HEAD_PAIR
    return pl.pallas_call(
        functools.partial(_mb_kernel, nb=nb),
        out_shape=jax.ShapeDtypeStruct((b, s, MB_WIDTH), BF16),
        grid=(b, npair, nb),
        in_specs=[pl.BlockSpec((1, t, LANES), lambda bi, p, i: (bi, i, q_col + p)),
                  pl.BlockSpec((1, s, LANES), lambda bi, p, i: (bi, 0, k_col + p)),
                  pl.BlockSpec((1, s, LANES), lambda bi, p, i: (bi, 0, v_col + p)),
                  pl.BlockSpec((HEAD_PAIR, 3, t, t), lambda bi, p, i: (p, 0, 0, 0))],
        out_specs=pl.BlockSpec((1, t, LANES), lambda bi, p, i: (bi, i, p)),
        scratch_shapes=[pltpu.VMEM((LANES, LANES), F32)],
        compiler_params=_cparams(3),
        name="mb_attn",
    )(qkv, qkv, qkv, bias_tab)


def _t5_bucket(dist):
    n = jnp.maximum(dist, 0)
    max_exact = REL_BUCKETS // 2
    nf = jnp.maximum(n, 1).astype(F32)
    large = max_exact + (jnp.log(nf / max_exact) / jnp.log(jnp.float32(REL_MAX_DIST / max_exact))
                         * (REL_BUCKETS - max_exact)).astype(I32)
    large = jnp.minimum(large, REL_BUCKETS - 1)
    return jnp.where(n < max_exact, n, large)


def _moba_bias_table(rel_bias):
    assert MB_BLOCK >= REL_MAX_DIST
    t = ATT_TILE
    d = jnp.arange(t)[:, None] - jnp.arange(t)[None, :]
    own = jnp.where(d[None] >= 0, rel_bias[:, _t5_bucket(d)], NEG)
    prev = rel_bias[:, _t5_bucket(d + t)]
    far = jnp.broadcast_to(rel_bias[:, REL_BUCKETS - 1][:, None, None], prev.shape)
    return jnp.stack([own, prev, far], axis=1).astype(F32)


def _out_proj_kernel(osb_ref, omb_ref, gate_ref, x_ref, mod_ref, g2_ref, wsb_ref, wmb_ref, wout_ref,
                     x1_ref, h2_ref):
    d = x_ref.shape[-1]
    gates = gate_ref[0].astype(F32)
    merged = (gates[:, :d] * _dot(osb_ref[0], wsb_ref[...])
              + gates[:, d:] * _dot(omb_ref[0], wmb_ref[...]))
    x1 = x_ref[0] + mod_ref[0, 2:3, :] * _dot(merged.astype(BF16), wout_ref[...])
    x1_ref[0] = x1
    h2_ref[0] = _rms_mod(x1, g2_ref[...], mod_ref[0, 3:4, :], mod_ref[0, 4:5, :])


def _out_proj(osb, omb, gates, x, mod, g2, wsb, wmb, wout, tm=512):
    b, s, d = x.shape
    row = lambda w: pl.BlockSpec((1, tm, w), lambda bi, i: (bi, i, 0))
    full = lambda a: pl.BlockSpec(a.shape, lambda bi, i: (0,) * a.ndim)
    return pl.pallas_call(
        _out_proj_kernel,
        out_shape=(jax.ShapeDtypeStruct((b, s, d), F32), jax.ShapeDtypeStruct((b, s, d), F32)),
        grid=(b, s // tm),
        in_specs=[row(osb.shape[-1]), row(omb.shape[-1]), row(gates.shape[-1]), row(d),
                  pl.BlockSpec((1, N_MOD, d), lambda bi, i: (bi, 0, 0)),
                  full(g2), full(wsb), full(wmb), full(wout)],
        out_specs=(row(d), row(d)),
        compiler_params=_cparams(2),
        name="out_proj",
    )(osb, omb, gates, x, mod, g2, wsb, wmb, wout)


def _first_argmax(v, iota, n):
    m = jnp.max(v, axis=0, keepdims=True)
    idx = jnp.min(jnp.where(v == m, iota, n), axis=0, keepdims=True)
    return m, idx


def _route_kernel(h_ref, wr_ref, rb_ref, eidx_ref, wts_ref, slot_ref, cnt_ref, carry_ref):
    tm = h_ref.shape[0]
    e, g, gs = N_EXPERTS, N_GROUPS, GROUP_SIZE
    step = pl.program_id(0)

    @pl.when(step == 0)
    def _():
        carry_ref[...] = jnp.zeros_like(carry_ref)

    h0, h1, _ = _split3(h_ref[...])
    w0, w1, _ = _split3(wr_ref[...])
    logits = _dot_nt(w0, h0) + _dot_nt(w0, h1) + _dot_nt(w1, h0)
    scores = _sigmoid(logits)
    choice = scores + rb_ref[...]

    c3 = choice.reshape(g, gs, tm)
    sub = lax.broadcasted_iota(I32, (g, gs, tm), 1)
    m1 = jnp.max(c3, axis=1, keepdims=True)
    i1 = jnp.min(jnp.where(c3 == m1, sub, gs), axis=1, keepdims=True)
    m2 = jnp.max(jnp.where(sub == i1, -jnp.inf, c3), axis=1, keepdims=True)
    gscore = (m1 + m2).reshape(g, tm)

    giota = lax.broadcasted_iota(I32, (g, tm), 0)
    gmask = jnp.zeros((g, tm), F32)
    for _ in range(TOPK_GROUPS):
        _, gi = _first_argmax(gscore, giota, g)
        pick = giota == gi
        gmask = jnp.where(pick, 1.0, gmask)
        gscore = jnp.where(pick, -jnp.inf, gscore)
    emask = jnp.broadcast_to(gmask.reshape(g, 1, tm), (g, gs, tm)).reshape(e, tm)
    masked = jnp.where(emask > 0.5, choice, NEG)

    eiota = lax.broadcasted_iota(I32, (e, tm), 0)
    chosen = jnp.zeros((e, tm), F32)
    idxs, ws = [], []
    for _ in range(TOPK_EXPERTS):
        _, ei = _first_argmax(masked, eiota, e)
        pick = eiota == ei
        idxs.append(ei)
        ws.append(jnp.sum(jnp.where(pick, scores, 0.0), axis=0, keepdims=True))
        chosen = jnp.where(pick, 1.0, chosen)
        masked = jnp.where(pick, -jnp.inf, masked)
    wsum = ws[0]
    for w in ws[1:]:
        wsum = wsum + w

    r = lax.broadcasted_iota(I32, (tm, tm), 0)
    c = lax.broadcasted_iota(I32, (tm, tm), 1)
    before = jnp.where(r < c, 1.0, 0.0).astype(BF16)
    prefix = _dot(chosen.astype(BF16), before) + carry_ref[...]
    for k in range(TOPK_EXPERTS):
        eidx_ref[k:k + 1, :] = idxs[k]
        wts_ref[k:k + 1, :] = ws[k] / wsum * ROUTED_SCALE
        slot = jnp.sum(jnp.where(eiota == idxs[k], prefix, 0.0), axis=0, keepdims=True)
        slot_ref[k:k + 1, :] = slot.astype(I32)
    carry_ref[...] = carry_ref[...] + jnp.sum(chosen, axis=1, keepdims=True)
    cnt_ref[...] = carry_ref[...].astype(I32)


def _route(h2, w_router_t, router_bias, tm=512):
    t, d = h2.shape
    e = N_EXPERTS
    kk = TOPK_EXPERTS
    tok = lambda: pl.BlockSpec((kk, tm), lambda i: (0, i))
    return pl.pallas_call(
        _route_kernel,
        out_shape=(jax.ShapeDtypeStruct((kk, t), I32), jax.ShapeDtypeStruct((kk, t), F32),
                   jax.ShapeDtypeStruct((kk, t), I32), jax.ShapeDtypeStruct((e, 1), I32)),
        grid=(t // tm,),
        in_specs=[pl.BlockSpec((tm, d), lambda i: (i, 0)),
                  pl.BlockSpec((e, d), lambda i: (0, 0)),
                  pl.BlockSpec((e, 1), lambda i: (0, 0))],
        out_specs=(tok(), tok(), tok(), pl.BlockSpec((e, 1), lambda i: (0, 0))),
        scratch_shapes=[pltpu.VMEM((e, 1), F32)],
        compiler_params=_cparams(1),
        name="route",
    )(h2, w_router_t, router_bias.reshape(e, 1))


def _dispatch_kernel(dest_hbm, h_ref, xs_in, xs_hbm, dest_smem, sem_idx, sem_row):
    del xs_in
    tb = h_ref.shape[0]
    step = pl.program_id(0)
    idx_copy = pltpu.make_async_copy(dest_hbm.at[step], dest_smem, sem_idx)
    idx_copy.start()
    idx_copy.wait()

    def row_copy(k, r):
        return pltpu.make_async_copy(h_ref.at[pl.ds(r, 1), :],
                                     xs_hbm.at[pl.ds(dest_smem[k, r], 1), :], sem_row)

    def issue(r, carry):
        for k in range(TOPK_EXPERTS):
            row_copy(k, r).start()
        return carry

    def drain(r, carry):
        for k in range(TOPK_EXPERTS):
            row_copy(k, r).wait()
        return carry

    lax.fori_loop(0, tb, issue, 0)
    lax.fori_loop(0, tb, drain, 0)


def _dispatch(dest_blocks, h2, n_rows):
    t, d = h2.shape
    nsteps, kk, tb = dest_blocks.shape
    xs0 = jnp.zeros((n_rows, d), F32)
    return pl.pallas_call(
        _dispatch_kernel,
        out_shape=jax.ShapeDtypeStruct((n_rows, d), F32),
        grid=(nsteps,),
        in_specs=[pl.BlockSpec(memory_space=pl.ANY),
                  pl.BlockSpec((tb, d), lambda i: (i, 0)),
                  pl.BlockSpec(memory_space=pl.ANY)],
        out_specs=pl.BlockSpec(memory_space=pl.ANY),
        scratch_shapes=[pltpu.SMEM((kk, tb), I32), pltpu.SemaphoreType.DMA, pltpu.SemaphoreType.DMA],
        input_output_aliases={2: 0},
        compiler_params=_cparams(1),
        name="dispatch",
    )(dest_blocks, h2, xs0)


def _ffn_kernel(be_ref, nused_ref, xs_ref, wg_ref, wu_ref, wd_ref, ys_ref):
    del be_ref
    i = pl.program_id(0)

    @pl.when(i < nused_ref[0])
    def _():
        x = xs_ref[...].astype(BF16)
        a = _dot(x, wg_ref[0])
        u = _dot(x, wu_ref[0])
        act = (a * _sigmoid(a) * u).astype(BF16)
        ys_ref[...] = _dot(act, wd_ref[0])

    @pl.when(i >= nused_ref[0])
    def _():
        ys_ref[...] = jnp.zeros_like(ys_ref)


def _ffn(blk_expert, n_used, xs, wg, wu, wd):
    n_rows, d = xs.shape
    rb = MOE_ROW_BLOCK
    ff = wg.shape[-1]
    return pl.pallas_call(
        _ffn_kernel,
        out_shape=jax.ShapeDtypeStruct((n_rows, d), F32),
        grid_spec=pltpu.PrefetchScalarGridSpec(
            num_scalar_prefetch=2,
            grid=(n_rows // rb,),
            in_specs=[pl.BlockSpec((rb, d), lambda i, be, nu: (i, 0)),
                      pl.BlockSpec((1, d, ff), lambda i, be, nu: (be[i], 0, 0)),
                      pl.BlockSpec((1, d, ff), lambda i, be, nu: (be[i], 0, 0)),
                      pl.BlockSpec((1, ff, d), lambda i, be, nu: (be[i], 0, 0))],
            out_specs=pl.BlockSpec((rb, d), lambda i, be, nu: (i, 0)),
        ),
        compiler_params=_cparams(1),
        name="ffn",
    )(blk_expert, n_used, xs, wg, wu, wd)


def _combine_kernel(dest_hbm, ys_hbm, wts_ref, h_ref, x1_ref, gate2_ref, fg_ref, wgs_ref, wus_ref, wds_ref,
                    o_ref, dest_smem, rows_ref, sem_idx, sem_row):
    tb = h_ref.shape[1]
    step = pl.program_id(0) * pl.num_programs(1) + pl.program_id(1)
    idx_copy = pltpu.make_async_copy(dest_hbm.at[step], dest_smem, sem_idx)
    idx_copy.start()
    idx_copy.wait()

    def row_copy(k, r):
        return pltpu.make_async_copy(ys_hbm.at[pl.ds(dest_smem[k, r], 1), :],
                                     rows_ref.at[k, pl.ds(r, 1), :], sem_row)

    def issue(r, carry):
        for k in range(TOPK_EXPERTS):
            row_copy(k, r).start()
        return carry

    def drain(r, carry):
        for k in range(TOPK_EXPERTS):
            row_copy(k, r).wait()
        return carry

    lax.fori_loop(0, tb, issue, 0)

    hb = h_ref[0].astype(BF16)
    a = _dot(hb, wgs_ref[...])
    u = _dot(hb, wus_ref[...])
    y = _dot((a * _sigmoid(a) * u).astype(BF16), wds_ref[...])

    lax.fori_loop(0, tb, drain, 0)
    w = wts_ref[0]
    for k in range(TOPK_EXPERTS):
        y = y + w[:, k:k + 1] * rows_ref[k]
    x2 = x1_ref[0] + gate2_ref[0] * y
    o_ref[0] = x2 * lax.rsqrt(jnp.mean(x2 * x2, axis=-1, keepdims=True) + EPS) * fg_ref[...]


def _combine(dest_blocks, ys, wts, h2, x1, gate2, final_g, wgs, wus, wds):
    b, s, d = x1.shape
    nsteps, kk, tb = dest_blocks.shape
    per_b = s // tb
    row = lambda w: pl.BlockSpec((1, tb, w), lambda bi, i: (bi, i, 0))
    full = lambda a: pl.BlockSpec(a.shape, lambda bi, i: (0,) * a.ndim)
    return pl.pallas_call(
        _combine_kernel,
        out_shape=jax.ShapeDtypeStruct((b, s, d), F32),
        grid=(b, per_b),
        in_specs=[pl.BlockSpec(memory_space=pl.ANY),
                  pl.BlockSpec(memory_space=pl.ANY),
                  row(kk), row(d), row(d),
                  pl.BlockSpec((1, 1, d), lambda bi, i: (bi, 0, 0)),
                  full(final_g), full(wgs), full(wus), full(wds)],
        out_specs=row(d),
        scratch_shapes=[pltpu.SMEM((kk, tb), I32), pltpu.VMEM((kk, tb, d), F32),
                        pltpu.SemaphoreType.DMA, pltpu.SemaphoreType.DMA],
        compiler_params=_cparams(2),
        name="combine",
    )(dest_blocks, ys, wts, h2, x1, gate2, final_g, wgs, wus, wds)


def _layer(x, mod, norm1_g, norm2_g, w_in, w_branch_sb, w_branch_mb, w_out, w_router, router_bias,
           w_gate_e, w_up_e, w_down_e, w_gate_sh, w_up_sh, w_down_sh, bias_tab, final_g):
    b, s, d = x.shape
    t = b * s
    nqkv = 3 * SB_WIDTH + 3 * MB_WIDTH

    col = jnp.arange(nqkv)
    is_q = (col < SB_WIDTH) | ((col >= 3 * SB_WIDTH) & (col < 3 * SB_WIDTH + MB_WIDTH))
    wqkv = (w_in[:, :nqkv] * jnp.where(is_q, HEAD_DIM ** -0.5, 1.0)).astype(BF16)
    wg = w_in[:, nqkv:].astype(BF16)
    qkv, gates = _in_proj(x, mod, norm1_g.reshape(1, d), wqkv, wg)

    cb = SB_WIDTH // LANES
    o_sb = _sb_attn(qkv, 0, cb, 2 * cb)
    o_mb = _mb_attn(qkv, bias_tab, 3 * cb, 4 * cb, 5 * cb)

    x1, h2 = _out_proj(o_sb, o_mb, gates, x, mod, norm2_g.reshape(1, d), w_branch_sb.astype(BF16),
                       w_branch_mb.astype(BF16), w_out.astype(BF16))

    h2f = h2.reshape(t, d)
    eidx, wts, slot, counts = _route(h2f, w_router.T, router_bias)

    rb = MOE_ROW_BLOCK
    counts = counts.reshape(N_EXPERTS)
    pcounts = (counts + rb - 1) // rb * rb
    pend = jnp.cumsum(pcounts)
    pstart = pend - pcounts
    n_blk = t * TOPK_EXPERTS // rb + N_EXPERTS
    blk_expert = jnp.minimum(
        jnp.searchsorted(pend, jnp.arange(n_blk, dtype=I32) * rb, side="right"), N_EXPERTS - 1).astype(I32)
    n_used = (pend[-1:] // rb).astype(I32)
    dest = pstart[eidx] + slot
    tb = 256
    dest_blocks = dest.reshape(TOPK_EXPERTS, t // tb, tb).transpose(1, 0, 2)

    xs = _dispatch(dest_blocks, h2f, n_blk * rb)
    ys = _ffn(blk_expert, n_used, xs, w_gate_e.astype(BF16), w_up_e.astype(BF16), w_down_e.astype(BF16))
    wts_tok = wts.T.reshape(b, s, TOPK_EXPERTS)
    return _combine(dest_blocks, ys, wts_tok, h2, x1, mod[:, 5:6, :], final_g.reshape(1, d), w_gate_sh.astype(BF16),
                    w_up_sh.astype(BF16), w_down_sh.astype(BF16))


def kernel(x, c, norm1_g, norm2_g, w_ada, b_ada, w_in, w_branch_sb, w_branch_mb, w_out, w_router, router_bias,
           w_gate_e, w_up_e, w_down_e, w_gate_sh, w_up_sh, w_down_sh, rel_bias, final_g):
    b, s, d = x.shape
    depth = w_ada.shape[0]
    assert depth == 1, "the final norm is fused into the only layer's combine step"
    bias_tab = _moba_bias_table(rel_bias)
    l = 0
    mod = _ada(c, w_ada[l], b_ada[l]).reshape(b, N_MOD, d)
    return _layer(x, mod, norm1_g[l], norm2_g[l], w_in[l], w_branch_sb[l], w_branch_mb[l], w_out[l],
                  w_router[l], router_bias[l], w_gate_e[l], w_up_e[l], w_down_e[l], w_gate_sh[l],
                  w_up_sh[l], w_down_sh[l], bias_tab, final_g)
```

```python
import functools

import jax
import jax.numpy as jnp
from jax import lax
from jax.experimental import pallas as pl
from jax.experimental.pallas import tpu as pltpu

F32 = jnp.float32
BF16 = jnp.bfloat16
I32 = jnp.int32
U32 = jnp.uint32

HEAD_DIM = 64
SB_HEADS = 8
MB_HEADS = 8
SB_WIDTH = SB_HEADS * HEAD_DIM
MB_WIDTH = MB_HEADS * HEAD_DIM
MB_BLOCK = 256
MB_TOPK = 3
REL_BUCKETS = 32
REL_MAX_DIST = 128
N_EXPERTS = 64
N_GROUPS = 8
GROUP_SIZE = N_EXPERTS // N_GROUPS
TOPK_GROUPS = 4
TOPK_EXPERTS = 8
ROUTED_SCALE = 2.5
MOE_ROW_BLOCK = 256
N_MOD = 6
EPS = 1e-6
NEG = -1e30

LANES = 128
HEAD_PAIR = LANES // HEAD_DIM
ATT_TILE = 256
SB_CUTOFF = 120.0
ONES_ROWS = 16
VMEM_LIMIT = 56 * 1024 * 1024


def _cparams(n_axes):
    return pltpu.CompilerParams(dimension_semantics=("arbitrary",) * n_axes,
                                vmem_limit_bytes=VMEM_LIMIT)


def _dot(a, b):
    return jnp.dot(a, b, preferred_element_type=F32)


def _dot_nt(a, b):
    return lax.dot_general(a, b, (((1,), (1,)), ((), ())), preferred_element_type=F32)


def _split3(a):
    p0 = a.astype(BF16)
    r0 = a - p0.astype(F32)
    p1 = r0.astype(BF16)
    p2 = (r0 - p1.astype(F32)).astype(BF16)
    return p0, p1, p2


def _sigmoid(x):
    return 1.0 / (1.0 + jnp.exp(-x))


def _rms_mod(x, g, shift, scale):
    y = x * lax.rsqrt(jnp.mean(x * x, axis=-1, keepdims=True) + EPS) * g
    return y * (1.0 + scale) + shift


def _ada_kernel(c_ref, w_ref, b_ref, o_ref):
    c = c_ref[...]
    s = c * _sigmoid(c)
    s0, s1, _ = _split3(s)
    w0, w1, _ = _split3(w_ref[...])
    o_ref[...] = _dot(s0, w0) + _dot(s0, w1) + _dot(s1, w0) + b_ref[...]


def _ada(c, w_ada, b_ada):
    b, d = c.shape
    n = w_ada.shape[1]
    tn = 1536
    return pl.pallas_call(
        _ada_kernel,
        out_shape=jax.ShapeDtypeStruct((b, n), F32),
        grid=(n // tn,),
        in_specs=[pl.BlockSpec((b, d), lambda j: (0, 0)),
                  pl.BlockSpec((d, tn), lambda j: (0, j)),
                  pl.BlockSpec((1, tn), lambda j: (0, j))],
        out_specs=pl.BlockSpec((b, tn), lambda j: (0, j)),
        compiler_params=_cparams(1),
        name="ada",
    )(c, w_ada, b_ada.reshape(1, n))


def _in_proj_kernel(x_ref, mod_ref, g_ref, wqkv_ref, wg_ref, qkv_ref, gate_ref):
    h = _rms_mod(x_ref[0], g_ref[...], mod_ref[0, 0:1, :], mod_ref[0, 1:2, :]).astype(BF16)
    qkv_ref[0] = _dot(h, wqkv_ref[...]).astype(BF16)
    gate_ref[0] = _sigmoid(_dot(h, wg_ref[...])).astype(BF16)


def _in_proj(x, mod, g, wqkv, wg, tm=512):
    b, s, d = x.shape
    nq, ng = wqkv.shape[1], wg.shape[1]
    return pl.pallas_call(
        _in_proj_kernel,
        out_shape=(jax.ShapeDtypeStruct((b, s, nq), BF16), jax.ShapeDtypeStruct((b, s, ng), BF16)),
        grid=(b, s // tm),
        in_specs=[pl.BlockSpec((1, tm, d), lambda bi, i: (bi, i, 0)),
                  pl.BlockSpec((1, N_MOD, d), lambda bi, i: (bi, 0, 0)),
                  pl.BlockSpec((1, d), lambda bi, i: (0, 0)),
                  pl.BlockSpec((d, nq), lambda bi, i: (0, 0)),
                  pl.BlockSpec((d, ng), lambda bi, i: (0, 0))],
        out_specs=(pl.BlockSpec((1, tm, nq), lambda bi, i: (bi, i, 0)),
                   pl.BlockSpec((1, tm, ng), lambda bi, i: (bi, i, 0))),
        compiler_params=_cparams(2),
        name="in_proj",
    )(x, mod, g, wqkv, wg)


def _head_mask(h):
    lane = lax.broadcasted_iota(I32, (ATT_TILE, LANES), 1)
    return (lane >= h * HEAD_DIM) & (lane < (h + 1) * HEAD_DIM)


def _sb_kernel(q_ref, k_ref, v_ref, o_ref):
    t = ATT_TILE
    i = pl.program_id(2)
    row = lax.broadcasted_iota(I32, (t, t), 0)
    col = lax.broadcasted_iota(I32, (t, t), 1)
    later = jnp.where(row > col, 1.0, 0.0).astype(BF16)
    past = col < row
    q = q_ref[0]
    qs = [jnp.where(_head_mask(h), q, jnp.zeros_like(q)) for h in range(HEAD_PAIR)]

    def tile(j, carries, accs, diag):
        start = pl.multiple_of(j * t, t)
        kj = k_ref[0, pl.ds(start, t), :]
        vj = v_ref[0, pl.ds(start, t), :]
        new_c, new_a = [], []
        for h in range(HEAD_PAIR):
            z = _dot_nt(qs[h], kj)
            sp = jnp.maximum(z, 0.0) + jnp.log(1.0 + jnp.exp(-jnp.abs(z)))
            log_rest = -sp
            if diag:
                log_rest = jnp.where(past, log_rest, 0.0)
            hi = log_rest.astype(BF16)
            lo = (log_rest - hi.astype(F32)).astype(BF16)
            log_skip = _dot(hi, later) + _dot(lo, later) + carries[h]
            a = jnp.exp(z - sp + log_skip)
            if diag:
                a = jnp.where(past, a, 0.0)
            new_a.append(accs[h] + _dot(a.astype(BF16), vj))
            new_c.append(carries[h] + jnp.sum(log_rest, axis=1, keepdims=True))
        return tuple(new_c), tuple(new_a)

    zero_c = jnp.zeros((t, 1), F32)
    zero_a = jnp.zeros((t, LANES), F32)
    carries, accs = tile(i, (zero_c,) * HEAD_PAIR, (zero_a,) * HEAD_PAIR, True)

    def live(st):
        j, carries, _ = st
        worst = carries[0]
        for c in carries[1:]:
            worst = jnp.maximum(worst, c)
        return (j >= 0) & (jnp.max(worst) > -SB_CUTOFF)

    def older(st):
        j, carries, accs = st
        carries, accs = tile(j, carries, accs, False)
        return j - 1, carries, accs

    _, _, accs = lax.while_loop(live, older, (i - 1, carries, accs))
    o_ref[0] = jnp.where(_head_mask(0), accs[0], accs[1]).astype(BF16)


def _sb_attn(qkv, q_col, k_col, v_col):
    b, s, _ = qkv.shape
    t = ATT_TILE
    npair = SB_HEADS // HEAD_PAIR
    return pl.pallas_call(
        _sb_kernel,
        out_shape=jax.ShapeDtypeStruct((b, s, SB_WIDTH), BF16),
        grid=(b, npair, s // t),
        in_specs=[pl.BlockSpec((1, t, LANES), lambda bi, p, i: (bi, i, q_col + p)),
                  pl.BlockSpec((1, s, LANES), lambda bi, p, i: (bi, 0, k_col + p)),
                  pl.BlockSpec((1, s, LANES), lambda bi, p, i: (bi, 0, v_col + p))],
        out_specs=pl.BlockSpec((1, t, LANES), lambda bi, p, i: (bi, i, p)),
        compiler_params=_cparams(3),
        name="sb_attn",
    )(qkv, qkv, qkv)


def _mb_kernel(qt_ref, k_ref, vt_ref, bias_ref, o_ref, kbar_ref, *, nb):
    t = ATT_TILE
    hd = HEAD_DIM
    i = pl.program_id(2)
    lane_row = lax.broadcasted_iota(I32, (1, LANES), 1)
    slot = lax.broadcasted_iota(I32, (LANES, t), 0)
    blk = lax.broadcasted_iota(I32, (hd, t), 0)

    @pl.when(i == 0)
    def _():
        kbar_ref[...] = jnp.zeros_like(kbar_ref)
        for n in range(nb):
            kbar_ref[n:n + 1, :] = jnp.mean(k_ref[0, n * t:(n + 1) * t, :].astype(F32), axis=0, keepdims=True)

    qt = qt_ref[0, 0]
    kb0, kb1, kb2 = _split3(kbar_ref[...])
    q_heads, not_sels = [], []
    for h in range(HEAD_PAIR):
        qh = jnp.where((slot >= h * hd) & (slot < (h + 1) * hd), qt, jnp.zeros_like(qt))
        gate = _dot(kb0, qh) + _dot(kb1, qh) + _dot(kb2, qh)
        gate = jnp.where(blk < i, gate, -jnp.inf)
        not_sel = jnp.ones((hd, t), F32)
        for _ in range(MB_TOPK):
            m = jnp.max(gate, axis=0, keepdims=True)
            idx = jnp.min(jnp.where(gate == m, blk, hd), axis=0, keepdims=True)
            pick = (blk == idx) & (m > -jnp.inf)
            not_sel = jnp.where(pick, 0.0, not_sel)
            gate = jnp.where(pick, -jnp.inf, gate)
        q_heads.append(qh)
        not_sels.append(not_sel.astype(BF16))
    q_ext = jnp.concatenate([jnp.concatenate(q_heads, axis=1), jnp.concatenate(not_sels, axis=1),
                             jnp.zeros((hd, HEAD_PAIR * t), BF16)], axis=0)
    ones_rows = jnp.ones((ONES_ROWS, t), BF16)

    def scores(j, masked):
        kj = k_ref[0, pl.ds(pl.multiple_of(j * t, t), t), :]
        pen = jnp.where(lane_row == j, NEG, 0.0) if masked else jnp.zeros((1, LANES), F32)
        return _dot(jnp.concatenate([kj, jnp.broadcast_to(pen.astype(BF16), (t, LANES))], axis=1), q_ext)

    def weighted(j, p):
        return _dot(jnp.concatenate([vt_ref[0, j], ones_rows], axis=0), p)

    def update(j, s, m_run, acc):
        m_new = jnp.maximum(m_run, jnp.max(s, axis=0, keepdims=True))
        p = jnp.exp(s - m_new).astype(BF16)
        return m_new, jnp.exp(m_run - m_new) * acc + weighted(j, p)

    s = scores(i, False) + bias_ref[0, 0]
    m_run = jnp.max(s, axis=0, keepdims=True)
    acc = weighted(i, jnp.exp(s - m_run).astype(BF16))
    i_prev = jnp.maximum(i - 1, 0)
    m_run, acc = update(i_prev, scores(i_prev, True) + bias_ref[0, 1], m_run, acc)

    def older(j, c):
        return update(j, scores(j, True), c[0], c[1])

    m_run, acc = lax.fori_loop(0, i - 1, older, (m_run, acc))
    denom = acc[LANES:LANES + 1, :]
    out_t = jnp.concatenate([acc[h * hd:(h + 1) * hd, h * t:(h + 1) * t] / denom[:, h * t:(h + 1) * t]
                             for h in range(HEAD_PAIR)], axis=0)
    o_ref[0] = out_t.T.astype(BF16)


def _bias_tab_kernel(rb_ref, bucket_ref, o_ref):
    h = pl.program_id(0)
    t = ATT_TILE
    far = rb_ref[h, REL_BUCKETS - 1]
    causal = lax.broadcasted_iota(I32, (t, t), 1) >= lax.broadcasted_iota(I32, (t, t), 0)
    for w in range(2):
        bk = bucket_ref[w]
        tile = jnp.zeros((t, t), F32)
        for b in range(REL_BUCKETS):
            tile = jnp.where(bk == b, rb_ref[h, b] - far, tile)
        o_ref[0, w] = jnp.where(causal, tile, NEG) if w == 0 else tile


def _moba_bias_table(rel_bias):
    assert MB_BLOCK >= REL_MAX_DIST and ATT_TILE == MB_BLOCK
    t = ATT_TILE
    nh = rel_bias.shape[0]
    d = jnp.arange(t, dtype=I32)[None, :] - jnp.arange(t, dtype=I32)[:, None]
    bucket = jnp.stack([_t5_bucket(d), _t5_bucket(d + t)]).astype(I32)
    return pl.pallas_call(
        _bias_tab_kernel,
        out_shape=jax.ShapeDtypeStruct((nh // HEAD_PAIR, 2, t, HEAD_PAIR * t), F32),
        grid=(nh,),
        in_specs=[pl.BlockSpec(memory_space=pltpu.SMEM),
                  pl.BlockSpec((2, t, t), lambda h: (0, 0, 0))],
        out_specs=pl.BlockSpec((1, 2, t, t), lambda h: (h // HEAD_PAIR, 0, 0, h % HEAD_PAIR)),
        compiler_params=_cparams(1),
        name="bias_tab",
    )(rel_bias, bucket)


def _blocked_t(a):
    b, s, w = a.shape
    return a.reshape(b, s // ATT_TILE, ATT_TILE, w).transpose(0, 1, 3, 2)


def _mb_attn(qkv, bias_tab, q_col, k_col, v_col):
    b, s, _ = qkv.shape
    t = ATT_TILE
    nb = s // t
    assert t == MB_BLOCK and s % t == 0 and nb <= HEAD_DIM and HEAD_PAIR == 2
    npair = MB_HEADS // HEAD_PAIR
    qt = _blocked_t(qkv[:, :, q_col * LANES:q_col * LANES + MB_WIDTH])
    vt = _blocked_t(qkv[:, :, v_col * LANES:v_col * LANES + MB_WIDTH])
    return pl.pallas_call(
        functools.partial(_mb_kernel, nb=nb),
        out_shape=jax.ShapeDtypeStruct((b, s, MB_WIDTH), BF16),
        grid=(b, npair, nb),
        in_specs=[pl.BlockSpec((1, 1, LANES, t), lambda bi, p, i: (bi, i, p, 0)),
                  pl.BlockSpec((1, s, LANES), lambda bi, p, i: (bi, 0, k_col + p)),
                  pl.BlockSpec((1, nb, LANES, t), lambda bi, p, i: (bi, 0, p, 0)),
                  pl.BlockSpec((1, 2, t, HEAD_PAIR * t), lambda bi, p, i: (p, 0, 0, 0))],
        out_specs=pl.BlockSpec((1, t, LANES), lambda bi, p, i: (bi, i, p)),
        scratch_shapes=[pltpu.VMEM((HEAD_DIM, LANES), F32)],
        compiler_params=_cparams(3),
        name="mb_attn",
    )(qt, qkv, vt, bias_tab)


def _t5_bucket(dist):
    n = jnp.maximum(dist, 0)
    max_exact = REL_BUCKETS // 2
    nf = jnp.maximum(n, 1).astype(F32)
    large = max_exact + (jnp.log(nf / max_exact) / jnp.log(jnp.float32(REL_MAX_DIST / max_exact))
                         * (REL_BUCKETS - max_exact)).astype(I32)
    large = jnp.minimum(large, REL_BUCKETS - 1)
    return jnp.where(n < max_exact, n, large)


def _out_proj_kernel(osb_ref, omb_ref, gate_ref, x_ref, mod_ref, g2_ref, wsb_ref, wmb_ref, wout_ref,
                     x1_ref, h2_ref, h2p_ref):
    d = x_ref.shape[-1]
    gates = gate_ref[0].astype(F32)
    merged = (gates[:, :d] * _dot(osb_ref[0], wsb_ref[...])
              + gates[:, d:] * _dot(omb_ref[0], wmb_ref[...]))
    x1 = x_ref[0] + mod_ref[0, 2:3, :] * _dot(merged.astype(BF16), wout_ref[...])
    x1_ref[0] = x1
    h2 = _rms_mod(x1, g2_ref[...], mod_ref[0, 3:4, :], mod_ref[0, 4:5, :])
    h2_ref[0] = h2
    h2p_ref[0] = _pack_halves(h2)


def _out_proj(osb, omb, gates, x, mod, g2, wsb, wmb, wout, tm=512):
    b, s, d = x.shape
    row = lambda w: pl.BlockSpec((1, tm, w), lambda bi, i: (bi, i, 0))
    full = lambda a: pl.BlockSpec(a.shape, lambda bi, i: (0,) * a.ndim)
    return pl.pallas_call(
        _out_proj_kernel,
        out_shape=(jax.ShapeDtypeStruct((b, s, d), F32), jax.ShapeDtypeStruct((b, s, d), F32),
                   jax.ShapeDtypeStruct((b, s, d // 2), U32)),
        grid=(b, s // tm),
        in_specs=[row(osb.shape[-1]), row(omb.shape[-1]), row(gates.shape[-1]), row(d),
                  pl.BlockSpec((1, N_MOD, d), lambda bi, i: (bi, 0, 0)),
                  full(g2), full(wsb), full(wmb), full(wout)],
        out_specs=(row(d), row(d), row(d // 2)),
        compiler_params=_cparams(2),
        name="out_proj",
    )(osb, omb, gates, x, mod, g2, wsb, wmb, wout)


def _first_argmax(v, iota, n):
    m = jnp.max(v, axis=0, keepdims=True)
    idx = jnp.min(jnp.where(v == m, iota, n), axis=0, keepdims=True)
    return m, idx


def _route_kernel(h_ref, wr_ref, rb_ref, eidx_ref, wts_ref, slot_ref, cnt_ref, carry_ref):
    tm = h_ref.shape[0]
    e, g, gs = N_EXPERTS, N_GROUPS, GROUP_SIZE
    step = pl.program_id(0)

    @pl.when(step == 0)
    def _():
        carry_ref[...] = jnp.zeros_like(carry_ref)

    h0, h1, _ = _split3(h_ref[...])
    w0, w1, _ = _split3(wr_ref[...])
    logits = _dot_nt(w0, h0) + _dot_nt(w0, h1) + _dot_nt(w1, h0)
    scores = _sigmoid(logits)
    choice = scores + rb_ref[...]

    c3 = choice.reshape(g, gs, tm)
    sub = lax.broadcasted_iota(I32, (g, gs, tm), 1)
    m1 = jnp.max(c3, axis=1, keepdims=True)
    i1 = jnp.min(jnp.where(c3 == m1, sub, gs), axis=1, keepdims=True)
    m2 = jnp.max(jnp.where(sub == i1, -jnp.inf, c3), axis=1, keepdims=True)
    gscore = (m1 + m2).reshape(g, tm)

    giota = lax.broadcasted_iota(I32, (g, tm), 0)
    gmask = jnp.zeros((g, tm), F32)
    for _ in range(TOPK_GROUPS):
        _, gi = _first_argmax(gscore, giota, g)
        pick = giota == gi
        gmask = jnp.where(pick, 1.0, gmask)
        gscore = jnp.where(pick, -jnp.inf, gscore)
    emask = jnp.broadcast_to(gmask.reshape(g, 1, tm), (g, gs, tm)).reshape(e, tm)
    masked = jnp.where(emask > 0.5, choice, NEG)

    eiota = lax.broadcasted_iota(I32, (e, tm), 0)
    chosen = jnp.zeros((e, tm), F32)
    idxs, ws = [], []
    for _ in range(TOPK_EXPERTS):
        _, ei = _first_argmax(masked, eiota, e)
        pick = eiota == ei
        idxs.append(ei)
        ws.append(jnp.sum(jnp.where(pick, scores, 0.0), axis=0, keepdims=True))
        chosen = jnp.where(pick, 1.0, chosen)
        masked = jnp.where(pick, -jnp.inf, masked)
    wsum = ws[0]
    for w in ws[1:]:
        wsum = wsum + w

    r = lax.broadcasted_iota(I32, (tm, tm), 0)
    c = lax.broadcasted_iota(I32, (tm, tm), 1)
    before = jnp.where(r < c, 1.0, 0.0).astype(BF16)
    prefix = _dot(chosen.astype(BF16), before) + carry_ref[...]
    for k in range(TOPK_EXPERTS):
        eidx_ref[k:k + 1, :] = idxs[k]
        wts_ref[k:k + 1, :] = ws[k] / wsum * ROUTED_SCALE
        slot = jnp.sum(jnp.where(eiota == idxs[k], prefix, 0.0), axis=0, keepdims=True)
        slot_ref[k:k + 1, :] = slot.astype(I32)
    carry_ref[...] = carry_ref[...] + jnp.sum(chosen, axis=1, keepdims=True)
    cnt_ref[...] = carry_ref[...].astype(I32)


def _route(h2, w_router_t, router_bias, tm=512):
    t, d = h2.shape
    e = N_EXPERTS
    kk = TOPK_EXPERTS
    tok = lambda: pl.BlockSpec((kk, tm), lambda i: (0, i))
    return pl.pallas_call(
        _route_kernel,
        out_shape=(jax.ShapeDtypeStruct((kk, t), I32), jax.ShapeDtypeStruct((kk, t), F32),
                   jax.ShapeDtypeStruct((kk, t), I32), jax.ShapeDtypeStruct((e, 1), I32)),
        grid=(t // tm,),
        in_specs=[pl.BlockSpec((tm, d), lambda i: (i, 0)),
                  pl.BlockSpec((e, d), lambda i: (0, 0)),
                  pl.BlockSpec((e, 1), lambda i: (0, 0))],
        out_specs=(tok(), tok(), tok(), pl.BlockSpec((e, 1), lambda i: (0, 0))),
        scratch_shapes=[pltpu.VMEM((e, 1), F32)],
        compiler_params=_cparams(1),
        name="route",
    )(h2, w_router_t, router_bias.reshape(e, 1))


def _dest_kernel(eidx_ref, slot_ref, pstart_ref, o_ref):
    kk, tm = eidx_ref.shape
    nsub, _, tb = o_ref.shape
    eiota = lax.broadcasted_iota(I32, (N_EXPERTS, tm), 0)
    rows = []
    for k in range(kk):
        base = jnp.sum(jnp.where(eiota == eidx_ref[k:k + 1, :], pstart_ref[...], 0), axis=0, keepdims=True)
        rows.append(base + slot_ref[k:k + 1, :])
    dest = jnp.concatenate(rows, axis=0)
    for s in range(nsub):
        o_ref[s] = dest[:, s * tb:(s + 1) * tb]


def _dest(eidx, slot, pstart, tb, tm=2048):
    kk, t = eidx.shape
    tm = min(tm, t)
    return pl.pallas_call(
        _dest_kernel,
        out_shape=jax.ShapeDtypeStruct((t // tb, kk, tb), I32),
        grid=(t // tm,),
        in_specs=[pl.BlockSpec((kk, tm), lambda i: (0, i)),
                  pl.BlockSpec((kk, tm), lambda i: (0, i)),
                  pl.BlockSpec((N_EXPERTS, 1), lambda i: (0, 0))],
        out_specs=pl.BlockSpec((tm // tb, kk, tb), lambda i: (i, 0, 0)),
        compiler_params=_cparams(1),
        name="dest",
    )(eidx, slot, pstart.reshape(N_EXPERTS, 1))


def _pack_halves(x):
    n = x.shape[1] // 2
    hi = lax.bitcast_convert_type(x[:, :n].astype(BF16).astype(F32), U32)
    lo = lax.bitcast_convert_type(x[:, n:].astype(BF16).astype(F32), U32)
    return hi | (lo >> 16)


def _unpack_halves(p):
    hi = lax.bitcast_convert_type(p & jnp.uint32(0xFFFF0000), F32)
    lo = lax.bitcast_convert_type(p << 16, F32)
    return hi, lo


def _dispatch_kernel(last_ref, has_ref, dest_hbm, h_ref, xs_hbm, dest_smem, zero_ref, sem_idx, sem_row, sem_zero):
    tb = h_ref.shape[0]
    rb = zero_ref.shape[0]
    step = pl.program_id(0)

    @pl.when(step == 0)
    def _():
        zero_ref[...] = jnp.zeros_like(zero_ref)

        def zero_copy(e):
            return pltpu.make_async_copy(zero_ref, xs_hbm.at[pl.ds(pl.multiple_of(last_ref[e], rb), rb), :],
                                         sem_zero)

        def zero_start(e, carry):
            @pl.when(has_ref[e] > 0)
            def _():
                zero_copy(e).start()
            return carry

        def zero_wait(e, carry):
            @pl.when(has_ref[e] > 0)
            def _():
                zero_copy(e).wait()
            return carry

        lax.fori_loop(0, N_EXPERTS, zero_start, 0)
        lax.fori_loop(0, N_EXPERTS, zero_wait, 0)

    idx_copy = pltpu.make_async_copy(dest_hbm.at[step], dest_smem, sem_idx)
    idx_copy.start()
    idx_copy.wait()

    def row_copy(k, r):
        return pltpu.make_async_copy(h_ref.at[pl.ds(r, 1), :],
                                     xs_hbm.at[pl.ds(dest_smem[k, r], 1), :], sem_row)

    def issue(r, carry):
        for k in range(TOPK_EXPERTS):
            row_copy(k, r).start()
        return carry

    def drain(r, carry):
        for k in range(TOPK_EXPERTS):
            row_copy(k, r).wait()
        return carry

    lax.fori_loop(0, tb, issue, 0)
    lax.fori_loop(0, tb, drain, 0)


def _dispatch(last_start, has_rows, dest_blocks, h2p, n_rows):
    t, w = h2p.shape
    nsteps, kk, tb = dest_blocks.shape
    rb = MOE_ROW_BLOCK
    return pl.pallas_call(
        _dispatch_kernel,
        out_shape=jax.ShapeDtypeStruct((n_rows, w), U32),
        grid_spec=pltpu.PrefetchScalarGridSpec(
            num_scalar_prefetch=2,
            grid=(nsteps,),
            in_specs=[pl.BlockSpec(memory_space=pl.ANY),
                      pl.BlockSpec((tb, w), lambda i, ls, hr: (i, 0))],
            out_specs=pl.BlockSpec(memory_space=pl.ANY),
            scratch_shapes=[pltpu.SMEM((kk, tb), I32), pltpu.VMEM((rb, w), U32),
                            pltpu.SemaphoreType.DMA, pltpu.SemaphoreType.DMA, pltpu.SemaphoreType.DMA],
        ),
        compiler_params=_cparams(1),
        name="dispatch",
    )(last_start, has_rows, dest_blocks, h2p)


def _ffn_kernel(be_ref, nused_ref, xs_ref, wg_ref, wu_ref, wd_ref, ys_ref, wgu_ref, wdn_ref):
    i = pl.program_id(0)
    ff = wg_ref.shape[-1]
    half = xs_ref.shape[-1]

    @pl.when((i == 0) | (be_ref[i] != be_ref[jnp.maximum(i - 1, 0)]))
    def _():
        wgu_ref[:, :ff] = wg_ref[0].astype(BF16)
        wgu_ref[:, ff:] = wu_ref[0].astype(BF16)
        wdn_ref[...] = wd_ref[0].astype(BF16)

    @pl.when(i < nused_ref[0])
    def _():
        x_hi, x_lo = _unpack_halves(xs_ref[...])
        au = _dot(x_hi.astype(BF16), wgu_ref[:half, :]) + _dot(x_lo.astype(BF16), wgu_ref[half:, :])
        a, u = au[:, :ff], au[:, ff:]
        act = (a * _sigmoid(a) * u).astype(BF16)
        ys_ref[...] = _pack_halves(_dot(act, wdn_ref[...]))

    @pl.when(i >= nused_ref[0])
    def _():
        ys_ref[...] = jnp.zeros_like(ys_ref)


def _ffn(blk_expert, n_used, xs, wg, wu, wd):
    n_rows, w = xs.shape
    rb = MOE_ROW_BLOCK
    _, d, ff = wg.shape
    return pl.pallas_call(
        _ffn_kernel,
        out_shape=jax.ShapeDtypeStruct((n_rows, w), U32),
        grid_spec=pltpu.PrefetchScalarGridSpec(
            num_scalar_prefetch=2,
            grid=(n_rows // rb,),
            in_specs=[pl.BlockSpec((rb, w), lambda i, be, nu: (i, 0)),
                      pl.BlockSpec((1, d, ff), lambda i, be, nu: (be[i], 0, 0)),
                      pl.BlockSpec((1, d, ff), lambda i, be, nu: (be[i], 0, 0)),
                      pl.BlockSpec((1, ff, d), lambda i, be, nu: (be[i], 0, 0))],
            out_specs=pl.BlockSpec((rb, w), lambda i, be, nu: (i, 0)),
            scratch_shapes=[pltpu.VMEM((d, 2 * ff), BF16), pltpu.VMEM((ff, d), BF16)],
        ),
        compiler_params=_cparams(1),
        name="ffn",
    )(blk_expert, n_used, xs, wg, wu, wd)


def _combine_kernel(dest_hbm, ys_hbm, wts_ref, h_ref, x1_ref, gate2_ref, fg_ref, wgus_ref, wds_ref,
                    o_ref, dest_smem, rows_ref, sem_idx, sem_row):
    tb = h_ref.shape[1]
    step = pl.program_id(0) * pl.num_programs(1) + pl.program_id(1)
    idx_copy = pltpu.make_async_copy(dest_hbm.at[step], dest_smem, sem_idx)
    idx_copy.start()
    idx_copy.wait()

    def row_copy(k, r):
        return pltpu.make_async_copy(ys_hbm.at[pl.ds(dest_smem[k, r], 1), :],
                                     rows_ref.at[k, pl.ds(r, 1), :], sem_row)

    def issue(r, carry):
        for k in range(TOPK_EXPERTS):
            row_copy(k, r).start()
        return carry

    def drain(r, carry):
        for k in range(TOPK_EXPERTS):
            row_copy(k, r).wait()
        return carry

    lax.fori_loop(0, tb, issue, 0)

    half = h_ref.shape[-1]
    ff = wds_ref.shape[0]
    h_hi, h_lo = _unpack_halves(h_ref[0])
    au = _dot(h_hi.astype(BF16), wgus_ref[:half, :]) + _dot(h_lo.astype(BF16), wgus_ref[half:, :])
    a, u = au[:, :ff], au[:, ff:]
    y = _dot((a * _sigmoid(a) * u).astype(BF16), wds_ref[...])

    lax.fori_loop(0, tb, drain, 0)
    w = wts_ref[0]
    y_hi, y_lo = y[:, :half], y[:, half:]
    for k in range(TOPK_EXPERTS):
        r_hi, r_lo = _unpack_halves(rows_ref[k])
        y_hi = y_hi + w[:, k:k + 1] * r_hi
        y_lo = y_lo + w[:, k:k + 1] * r_lo
    y = jnp.concatenate([y_hi, y_lo], axis=1)
    x2 = x1_ref[0] + gate2_ref[0] * y
    o_ref[0] = x2 * lax.rsqrt(jnp.mean(x2 * x2, axis=-1, keepdims=True) + EPS) * fg_ref[...]


def _combine(dest_blocks, ys, wts, h2p, x1, gate2, final_g, wgus, wds):
    b, s, d = x1.shape
    nsteps, kk, tb = dest_blocks.shape
    w = h2p.shape[-1]
    per_b = s // tb
    row = lambda n: pl.BlockSpec((1, tb, n), lambda bi, i: (bi, i, 0))
    full = lambda a: pl.BlockSpec(a.shape, lambda bi, i: (0,) * a.ndim)
    return pl.pallas_call(
        _combine_kernel,
        out_shape=jax.ShapeDtypeStruct((b, s, d), F32),
        grid=(b, per_b),
        in_specs=[pl.BlockSpec(memory_space=pl.ANY),
                  pl.BlockSpec(memory_space=pl.ANY),
                  row(kk), row(w), row(d),
                  pl.BlockSpec((1, 1, d), lambda bi, i: (bi, 0, 0)),
                  full(final_g), full(wgus), full(wds)],
        out_specs=row(d),
        scratch_shapes=[pltpu.SMEM((kk, tb), I32), pltpu.VMEM((kk, tb, w), U32),
                        pltpu.SemaphoreType.DMA, pltpu.SemaphoreType.DMA],
        compiler_params=_cparams(2),
        name="combine",
    )(dest_blocks, ys, wts, h2p, x1, gate2, final_g, wgus, wds)


def _layer(x, mod, norm1_g, norm2_g, w_in, w_branch_sb, w_branch_mb, w_out, w_router, router_bias,
           w_gate_e, w_up_e, w_down_e, w_gate_sh, w_up_sh, w_down_sh, bias_tab, final_g):
    b, s, d = x.shape
    t = b * s
    nqkv = 3 * SB_WIDTH + 3 * MB_WIDTH

    col = jnp.arange(nqkv)
    is_q = (col < SB_WIDTH) | ((col >= 3 * SB_WIDTH) & (col < 3 * SB_WIDTH + MB_WIDTH))
    wqkv = (w_in[:, :nqkv] * jnp.where(is_q, HEAD_DIM ** -0.5, 1.0)).astype(BF16)
    wg = w_in[:, nqkv:].astype(BF16)
    qkv, gates = _in_proj(x, mod, norm1_g.reshape(1, d), wqkv, wg)

    cb = SB_WIDTH // LANES
    o_sb = _sb_attn(qkv, 0, cb, 2 * cb)
    o_mb = _mb_attn(qkv, bias_tab, 3 * cb, 4 * cb, 5 * cb)

    x1, h2, h2p = _out_proj(o_sb, o_mb, gates, x, mod, norm2_g.reshape(1, d), w_branch_sb.astype(BF16),
                            w_branch_mb.astype(BF16), w_out.astype(BF16))

    eidx, wts, slot, counts = _route(h2.reshape(t, d), w_router.T, router_bias)

    rb = MOE_ROW_BLOCK
    counts = counts.reshape(N_EXPERTS)
    pcounts = (counts + rb - 1) // rb * rb
    pend = jnp.cumsum(pcounts)
    pstart = pend - pcounts
    n_blk = t * TOPK_EXPERTS // rb + N_EXPERTS
    blk_row = jnp.arange(n_blk, dtype=I32) * rb
    blk_expert = jnp.minimum(jnp.sum(pend[None, :] <= blk_row[:, None], axis=1), N_EXPERTS - 1).astype(I32)
    n_used = (pend[-1:] // rb).astype(I32)
    tb = 256
    dest_blocks = _dest(eidx, slot, pstart.astype(I32), tb)

    xs = _dispatch((pend - rb).astype(I32), pcounts.astype(I32), dest_blocks, h2p.reshape(t, d // 2), n_blk * rb)
    ys = _ffn(blk_expert, n_used, xs, w_gate_e, w_up_e, w_down_e)
    wts_tok = wts.T.reshape(b, s, TOPK_EXPERTS)
    wgus = jnp.concatenate([w_gate_sh, w_up_sh], axis=1).astype(BF16)
    return _combine(dest_blocks, ys, wts_tok, h2p, x1, mod[:, 5:6, :], final_g.reshape(1, d), wgus,
                    w_down_sh.astype(BF16))


def kernel(x, c, norm1_g, norm2_g, w_ada, b_ada, w_in, w_branch_sb, w_branch_mb, w_out, w_router, router_bias,
           w_gate_e, w_up_e, w_down_e, w_gate_sh, w_up_sh, w_down_sh, rel_bias, final_g):
    b, s, d = x.shape
    depth = w_ada.shape[0]
    assert depth == 1, "the final norm is fused into the only layer's combine step"
    bias_tab = _moba_bias_table(rel_bias)
    l = 0
    mod = _ada(c, w_ada[l], b_ada[l]).reshape(b, N_MOD, d)
    return _layer(x, mod, norm1_g[l], norm2_g[l], w_in[l], w_branch_sb[l], w_branch_mb[l], w_out[l],
                  w_router[l], router_bias[l], w_gate_e[l], w_up_e[l], w_down_e[l], w_gate_sh[l],
                  w_up_sh[l], w_down_sh[l], bias_tab, final_g)
```

```python
import functools

import jax
import jax.numpy as jnp
from jax import lax
from jax.experimental import pallas as pl
from jax.experimental.pallas import tpu as pltpu

F32 = jnp.float32
BF16 = jnp.bfloat16
I32 = jnp.int32
U32 = jnp.uint32

HEAD_DIM = 64
SB_HEADS = 8
MB_HEADS = 8
SB_WIDTH = SB_HEADS * HEAD_DIM
MB_WIDTH = MB_HEADS * HEAD_DIM
MB_BLOCK = 256
MB_TOPK = 3
REL_BUCKETS = 32
REL_MAX_DIST = 128
N_EXPERTS = 64
N_GROUPS = 8
GROUP_SIZE = N_EXPERTS // N_GROUPS
TOPK_GROUPS = 4
TOPK_EXPERTS = 8
ROUTED_SCALE = 2.5
N_MOD = 6
EPS = 1e-6
NEG = -1e30

LANES = 128
HEAD_PAIR = LANES // HEAD_DIM
ATT_TILE = 256
SB_CUTOFF = 120.0
ONES_ROWS = 16
ROW_BLOCK = 512
VMEM_LIMIT = 56 * 1024 * 1024


def _cparams(n_axes):
    return pltpu.CompilerParams(dimension_semantics=("arbitrary",) * n_axes,
                                vmem_limit_bytes=VMEM_LIMIT)


def _dot(a, b):
    return jnp.dot(a, b, preferred_element_type=F32)


def _dot_nt(a, b):
    return lax.dot_general(a, b, (((1,), (1,)), ((), ())), preferred_element_type=F32)


def _split3(a):
    p0 = a.astype(BF16)
    r0 = a - p0.astype(F32)
    p1 = r0.astype(BF16)
    p2 = (r0 - p1.astype(F32)).astype(BF16)
    return p0, p1, p2


def _sigmoid(x):
    return 1.0 / (1.0 + jnp.exp(-x))


def _rms_mod(x, g, shift, scale):
    y = x * lax.rsqrt(jnp.mean(x * x, axis=-1, keepdims=True) + EPS) * g
    return y * (1.0 + scale) + shift


def _ada_kernel(c_ref, w_ref, b_ref, o_ref):
    c = c_ref[...]
    s = c * _sigmoid(c)
    s0, s1, _ = _split3(s)
    w0, w1, _ = _split3(w_ref[...])
    o_ref[...] = _dot(s0, w0) + _dot(s0, w1) + _dot(s1, w0) + b_ref[...]


def _ada(c, w_ada, b_ada):
    b, d = c.shape
    n = w_ada.shape[1]
    tn = 1536
    return pl.pallas_call(
        _ada_kernel,
        out_shape=jax.ShapeDtypeStruct((b, n), F32),
        grid=(n // tn,),
        in_specs=[pl.BlockSpec((b, d), lambda j: (0, 0)),
                  pl.BlockSpec((d, tn), lambda j: (0, j)),
                  pl.BlockSpec((1, tn), lambda j: (0, j))],
        out_specs=pl.BlockSpec((b, tn), lambda j: (0, j)),
        compiler_params=_cparams(1),
        name="ada",
    )(c, w_ada, b_ada.reshape(1, n))


def _in_proj_kernel(x_ref, mod_ref, g_ref, wqkv_ref, wg_ref, qkv_ref, gate_ref):
    h = _rms_mod(x_ref[0], g_ref[...], mod_ref[0, 0:1, :], mod_ref[0, 1:2, :]).astype(BF16)
    qkv_ref[0] = _dot(h, wqkv_ref[...]).astype(BF16)
    gate_ref[0] = _sigmoid(_dot(h, wg_ref[...])).astype(BF16)


def _in_proj(x, mod, g, wqkv, wg, tm=512):
    b, s, d = x.shape
    nq, ng = wqkv.shape[1], wg.shape[1]
    return pl.pallas_call(
        _in_proj_kernel,
        out_shape=(jax.ShapeDtypeStruct((b, s, nq), BF16), jax.ShapeDtypeStruct((b, s, ng), BF16)),
        grid=(b, s // tm),
        in_specs=[pl.BlockSpec((1, tm, d), lambda bi, i: (bi, i, 0)),
                  pl.BlockSpec((1, N_MOD, d), lambda bi, i: (bi, 0, 0)),
                  pl.BlockSpec((1, d), lambda bi, i: (0, 0)),
                  pl.BlockSpec((d, nq), lambda bi, i: (0, 0)),
                  pl.BlockSpec((d, ng), lambda bi, i: (0, 0))],
        out_specs=(pl.BlockSpec((1, tm, nq), lambda bi, i: (bi, i, 0)),
                   pl.BlockSpec((1, tm, ng), lambda bi, i: (bi, i, 0))),
        compiler_params=_cparams(2),
        name="in_proj",
    )(x, mod, g, wqkv, wg)


def _head_mask(h):
    lane = lax.broadcasted_iota(I32, (ATT_TILE, LANES), 1)
    return (lane >= h * HEAD_DIM) & (lane < (h + 1) * HEAD_DIM)


def _sb_kernel(q_ref, k_ref, v_ref, o_ref):
    t = ATT_TILE
    rows = HEAD_PAIR * t
    i = pl.program_id(2)
    row = lax.broadcasted_iota(I32, (t, t), 0)
    col = lax.broadcasted_iota(I32, (t, t), 1)
    later = jnp.where(row > col, 1.0, 0.0).astype(BF16)
    past = (lax.broadcasted_iota(I32, (rows, t), 1)
            < lax.broadcasted_iota(I32, (rows, t), 0) % t)
    q = q_ref[0]
    q2 = jnp.concatenate([jnp.where(_head_mask(h), q, jnp.zeros_like(q)) for h in range(HEAD_PAIR)], axis=0)

    def tile(j, carry, acc, diag):
        start = pl.multiple_of(j * t, t)
        kj = k_ref[0, pl.ds(start, t), :]
        vj = v_ref[0, pl.ds(start, t), :]
        z = _dot_nt(q2, kj)
        sp = jnp.maximum(z, 0.0) + jnp.log(1.0 + jnp.exp(-jnp.abs(z)))
        log_rest = -sp
        if diag:
            log_rest = jnp.where(past, log_rest, 0.0)
        hi = log_rest.astype(BF16)
        lo = (log_rest - hi.astype(F32)).astype(BF16)
        cum = _dot(jnp.concatenate([hi, lo], axis=0), later)
        a = jnp.exp(z - sp + cum[:rows] + cum[rows:] + carry)
        if diag:
            a = jnp.where(past, a, 0.0)
        return carry + jnp.sum(log_rest, axis=1, keepdims=True), acc + _dot(a.astype(BF16), vj)

    carry, acc = tile(i, jnp.zeros((rows, 1), F32), jnp.zeros((rows, LANES), F32), True)

    def live(st):
        j, carry, _ = st
        return (j >= 0) & (jnp.max(carry) > -SB_CUTOFF)

    def older(st):
        j, carry, acc = st
        carry, acc = tile(j, carry, acc, False)
        return j - 1, carry, acc

    _, _, acc = lax.while_loop(live, older, (i - 1, carry, acc))
    o_ref[0] = jnp.where(_head_mask(0), acc[:t], acc[t:]).astype(BF16)


def _sb_attn(qkv, q_col, k_col, v_col):
    b, s, _ = qkv.shape
    t = ATT_TILE
    npair = SB_HEADS // HEAD_PAIR
    return pl.pallas_call(
        _sb_kernel,
        out_shape=jax.ShapeDtypeStruct((b, s, SB_WIDTH), BF16),
        grid=(b, npair, s // t),
        in_specs=[pl.BlockSpec((1, t, LANES), lambda bi, p, i: (bi, i, q_col + p)),
                  pl.BlockSpec((1, s, LANES), lambda bi, p, i: (bi, 0, k_col + p)),
                  pl.BlockSpec((1, s, LANES), lambda bi, p, i: (bi, 0, v_col + p))],
        out_specs=pl.BlockSpec((1, t, LANES), lambda bi, p, i: (bi, i, p)),
        compiler_params=_cparams(3),
        name="sb_attn",
    )(qkv, qkv, qkv)


def _mb_kernel(qt_ref, k_ref, vt_ref, bias_ref, o_ref, kbar_ref, *, nb):
    t = ATT_TILE
    hd = HEAD_DIM
    i = pl.program_id(2)
    lane_row = lax.broadcasted_iota(I32, (1, LANES), 1)
    slot = lax.broadcasted_iota(I32, (LANES, t), 0)
    blk = lax.broadcasted_iota(I32, (hd, t), 0)

    @pl.when(i == 0)
    def _():
        kbar_ref[...] = jnp.zeros_like(kbar_ref)
        for n in range(nb):
            kbar_ref[n:n + 1, :] = jnp.mean(k_ref[0, n * t:(n + 1) * t, :].astype(F32), axis=0, keepdims=True)

    qt = qt_ref[0, 0]
    kb0, kb1, kb2 = _split3(kbar_ref[...])
    q_heads, not_sels = [], []
    for h in range(HEAD_PAIR):
        qh = jnp.where((slot >= h * hd) & (slot < (h + 1) * hd), qt, jnp.zeros_like(qt))
        gate = _dot(kb0, qh) + _dot(kb1, qh) + _dot(kb2, qh)
        gate = jnp.where(blk < i, gate, -jnp.inf)
        not_sel = jnp.ones((hd, t), F32)
        for _ in range(MB_TOPK):
            m = jnp.max(gate, axis=0, keepdims=True)
            idx = jnp.min(jnp.where(gate == m, blk, hd), axis=0, keepdims=True)
            pick = (blk == idx) & (m > -jnp.inf)
            not_sel = jnp.where(pick, 0.0, not_sel)
            gate = jnp.where(pick, -jnp.inf, gate)
        q_heads.append(qh)
        not_sels.append(not_sel.astype(BF16))
    q_ext = jnp.concatenate([jnp.concatenate(q_heads, axis=1), jnp.concatenate(not_sels, axis=1),
                             jnp.ones((ONES_ROWS, HEAD_PAIR * t), BF16),
                             jnp.zeros((hd - ONES_ROWS, HEAD_PAIR * t), BF16)], axis=0)
    ones_rows = jnp.ones((ONES_ROWS, t), BF16)

    def scores(j, masked, dead=None):
        kj = k_ref[0, pl.ds(pl.multiple_of(j * t, t), t), :]
        pen = jnp.where(lane_row == j, NEG, 0.0) if masked else jnp.zeros((1, LANES), F32)
        if dead is not None:
            pen = pen + jnp.where(lane_row == hd, jnp.where(dead, NEG, 0.0), 0.0)
        return _dot(jnp.concatenate([kj, jnp.broadcast_to(pen.astype(BF16), (t, LANES))], axis=1), q_ext)

    def weighted(j, p):
        return _dot(jnp.concatenate([vt_ref[0, j], ones_rows], axis=0), p)

    def probs(s, m_run):
        m_new = jnp.maximum(m_run, jnp.max(s, axis=0, keepdims=True))
        return m_new, jnp.exp(s - m_new).astype(BF16)

    i_prev = jnp.maximum(i - 1, 0)
    s_a = scores(i, False) + bias_ref[0, 0]
    s_b = scores(i_prev, True) + bias_ref[0, 1]
    m_a, p_a = probs(s_a, jnp.full((1, HEAD_PAIR * t), NEG, F32))
    m_b, p_b = probs(s_b, jnp.full((1, HEAD_PAIR * t), NEG, F32))
    acc_a = weighted(i, p_a)
    acc_b = weighted(i_prev, p_b)

    def older(n, c):
        m_a, acc_a, m_b, acc_b = c
        j_a = 2 * n
        j_b = 2 * n + 1
        dead = j_b >= i - 1
        j_b = jnp.minimum(j_b, jnp.maximum(i - 2, 0))
        s_a = scores(j_a, True)
        s_b = scores(j_b, True, dead)
        m_a2, p_a = probs(s_a, m_a)
        m_b2, p_b = probs(s_b, m_b)
        acc_a = jnp.exp(m_a - m_a2) * acc_a + weighted(j_a, p_a)
        acc_b = jnp.exp(m_b - m_b2) * acc_b + weighted(j_b, p_b)
        return m_a2, acc_a, m_b2, acc_b

    m_a, acc_a, m_b, acc_b = lax.fori_loop(0, i // 2, older, (m_a, acc_a, m_b, acc_b))
    m_run = jnp.maximum(m_a, m_b)
    acc = jnp.exp(m_a - m_run) * acc_a + jnp.exp(m_b - m_run) * acc_b
    denom = acc[LANES:LANES + 1, :]
    out_t = jnp.concatenate([acc[h * hd:(h + 1) * hd, h * t:(h + 1) * t] / denom[:, h * t:(h + 1) * t]
                             for h in range(HEAD_PAIR)], axis=0)
    o_ref[0] = out_t.T.astype(BF16)


def _bias_tab_kernel(rb_ref, bucket_ref, o_ref):
    h = pl.program_id(0)
    t = ATT_TILE
    far = rb_ref[h, REL_BUCKETS - 1]
    causal = lax.broadcasted_iota(I32, (t, t), 1) >= lax.broadcasted_iota(I32, (t, t), 0)
    for w in range(2):
        bk = bucket_ref[w]
        tile = jnp.zeros((t, t), F32)
        for b in range(REL_BUCKETS):
            tile = jnp.where(bk == b, rb_ref[h, b] - far, tile)
        o_ref[0, w] = jnp.where(causal, tile, NEG) if w == 0 else tile


def _moba_bias_table(rel_bias):
    assert MB_BLOCK >= REL_MAX_DIST and ATT_TILE == MB_BLOCK
    t = ATT_TILE
    nh = rel_bias.shape[0]
    d = jnp.arange(t, dtype=I32)[None, :] - jnp.arange(t, dtype=I32)[:, None]
    bucket = jnp.stack([_t5_bucket(d), _t5_bucket(d + t)]).astype(I32)
    return pl.pallas_call(
        _bias_tab_kernel,
        out_shape=jax.ShapeDtypeStruct((nh // HEAD_PAIR, 2, t, HEAD_PAIR * t), F32),
        grid=(nh,),
        in_specs=[pl.BlockSpec(memory_space=pltpu.SMEM),
                  pl.BlockSpec((2, t, t), lambda h: (0, 0, 0))],
        out_specs=pl.BlockSpec((1, 2, t, t), lambda h: (h // HEAD_PAIR, 0, 0, h % HEAD_PAIR)),
        compiler_params=_cparams(1),
        name="bias_tab",
    )(rel_bias, bucket)


def _blocked_t(a):
    b, s, w = a.shape
    return a.reshape(b, s // ATT_TILE, ATT_TILE, w).transpose(0, 1, 3, 2)


def _mb_attn(qkv, bias_tab, q_col, k_col, v_col):
    b, s, _ = qkv.shape
    t = ATT_TILE
    nb = s // t
    assert t == MB_BLOCK and s % t == 0 and nb <= HEAD_DIM and HEAD_PAIR == 2
    npair = MB_HEADS // HEAD_PAIR
    qt = _blocked_t(qkv[:, :, q_col * LANES:q_col * LANES + MB_WIDTH])
    vt = _blocked_t(qkv[:, :, v_col * LANES:v_col * LANES + MB_WIDTH])
    return pl.pallas_call(
        functools.partial(_mb_kernel, nb=nb),
        out_shape=jax.ShapeDtypeStruct((b, s, MB_WIDTH), BF16),
        grid=(b, npair, nb),
        in_specs=[pl.BlockSpec((1, 1, LANES, t), lambda bi, p, i: (bi, i, p, 0)),
                  pl.BlockSpec((1, s, LANES), lambda bi, p, i: (bi, 0, k_col + p)),
                  pl.BlockSpec((1, nb, LANES, t), lambda bi, p, i: (bi, 0, p, 0)),
                  pl.BlockSpec((1, 2, t, HEAD_PAIR * t), lambda bi, p, i: (p, 0, 0, 0))],
        out_specs=pl.BlockSpec((1, t, LANES), lambda bi, p, i: (bi, i, p)),
        scratch_shapes=[pltpu.VMEM((HEAD_DIM, LANES), F32)],
        compiler_params=_cparams(3),
        name="mb_attn",
    )(qt, qkv, vt, bias_tab)


def _t5_bucket(dist):
    n = jnp.maximum(dist, 0)
    max_exact = REL_BUCKETS // 2
    nf = jnp.maximum(n, 1).astype(F32)
    large = max_exact + (jnp.log(nf / max_exact) / jnp.log(jnp.float32(REL_MAX_DIST / max_exact))
                         * (REL_BUCKETS - max_exact)).astype(I32)
    large = jnp.minimum(large, REL_BUCKETS - 1)
    return jnp.where(n < max_exact, n, large)


def _out_proj_kernel(osb_ref, omb_ref, gate_ref, x_ref, mod_ref, g2_ref, wsb_ref, wmb_ref, wout_ref,
                     x1_ref, h2_ref, h2p_ref):
    d = x_ref.shape[-1]
    gates = gate_ref[0].astype(F32)
    merged = (gates[:, :d] * _dot(osb_ref[0], wsb_ref[...])
              + gates[:, d:] * _dot(omb_ref[0], wmb_ref[...]))
    x1 = x_ref[0] + mod_ref[0, 2:3, :] * _dot(merged.astype(BF16), wout_ref[...])
    x1_ref[0] = x1
    h2 = _rms_mod(x1, g2_ref[...], mod_ref[0, 3:4, :], mod_ref[0, 4:5, :])
    h2_ref[0] = h2
    h2p_ref[0] = _pack_halves(h2)


def _out_proj(osb, omb, gates, x, mod, g2, wsb, wmb, wout, tm=512):
    b, s, d = x.shape
    row = lambda w: pl.BlockSpec((1, tm, w), lambda bi, i: (bi, i, 0))
    full = lambda a: pl.BlockSpec(a.shape, lambda bi, i: (0,) * a.ndim)
    return pl.pallas_call(
        _out_proj_kernel,
        out_shape=(jax.ShapeDtypeStruct((b, s, d), F32), jax.ShapeDtypeStruct((b, s, d), F32),
                   jax.ShapeDtypeStruct((b, s, d // 2), U32)),
        grid=(b, s // tm),
        in_specs=[row(osb.shape[-1]), row(omb.shape[-1]), row(gates.shape[-1]), row(d),
                  pl.BlockSpec((1, N_MOD, d), lambda bi, i: (bi, 0, 0)),
                  full(g2), full(wsb), full(wmb), full(wout)],
        out_specs=(row(d), row(d), row(d // 2)),
        compiler_params=_cparams(2),
        name="out_proj",
    )(osb, omb, gates, x, mod, g2, wsb, wmb, wout)


def _first_argmax(v, iota, n):
    m = jnp.max(v, axis=0, keepdims=True)
    idx = jnp.min(jnp.where(v == m, iota, n), axis=0, keepdims=True)
    return m, idx


def _route_kernel(h_ref, wr_ref, rb_ref, eidx_ref, wts_ref, slot_ref, cnt_ref, carry_ref):
    tm = h_ref.shape[0]
    e, g, gs = N_EXPERTS, N_GROUPS, GROUP_SIZE
    step = pl.program_id(0)

    @pl.when(step == 0)
    def _():
        carry_ref[...] = jnp.zeros_like(carry_ref)

    h0, h1, _ = _split3(h_ref[...])
    w0, w1, _ = _split3(wr_ref[...])
    logits = _dot_nt(w0, h0) + _dot_nt(w0, h1) + _dot_nt(w1, h0)
    scores = _sigmoid(logits)
    choice = scores + rb_ref[...]

    c3 = choice.reshape(g, gs, tm)
    sub = lax.broadcasted_iota(I32, (g, gs, tm), 1)
    m1 = jnp.max(c3, axis=1, keepdims=True)
    i1 = jnp.min(jnp.where(c3 == m1, sub, gs), axis=1, keepdims=True)
    m2 = jnp.max(jnp.where(sub == i1, -jnp.inf, c3), axis=1, keepdims=True)
    gscore = (m1 + m2).reshape(g, tm)

    giota = lax.broadcasted_iota(I32, (g, tm), 0)
    gmask = jnp.zeros((g, tm), F32)
    for _ in range(TOPK_GROUPS):
        _, gi = _first_argmax(gscore, giota, g)
        pick = giota == gi
        gmask = jnp.where(pick, 1.0, gmask)
        gscore = jnp.where(pick, -jnp.inf, gscore)
    emask = jnp.broadcast_to(gmask.reshape(g, 1, tm), (g, gs, tm)).reshape(e, tm)
    masked = jnp.where(emask > 0.5, choice, NEG)

    eiota = lax.broadcasted_iota(I32, (e, tm), 0)
    chosen = jnp.zeros((e, tm), F32)
    idxs, ws = [], []
    for _ in range(TOPK_EXPERTS):
        _, ei = _first_argmax(masked, eiota, e)
        pick = eiota == ei
        idxs.append(ei)
        ws.append(jnp.sum(jnp.where(pick, scores, 0.0), axis=0, keepdims=True))
        chosen = jnp.where(pick, 1.0, chosen)
        masked = jnp.where(pick, -jnp.inf, masked)
    wsum = ws[0]
    for w in ws[1:]:
        wsum = wsum + w

    r = lax.broadcasted_iota(I32, (tm, tm), 0)
    c = lax.broadcasted_iota(I32, (tm, tm), 1)
    before = jnp.where(r < c, 1.0, 0.0).astype(BF16)
    prefix = _dot(chosen.astype(BF16), before) + carry_ref[...]
    for k in range(TOPK_EXPERTS):
        eidx_ref[k:k + 1, :] = idxs[k]
        wts_ref[k:k + 1, :] = ws[k] / wsum * ROUTED_SCALE
        slot = jnp.sum(jnp.where(eiota == idxs[k], prefix, 0.0), axis=0, keepdims=True)
        slot_ref[k:k + 1, :] = slot.astype(I32)
    carry_ref[...] = carry_ref[...] + jnp.sum(chosen, axis=1, keepdims=True)
    cnt_ref[...] = carry_ref[...].astype(I32)


def _route(h2, w_router_t, router_bias, tm=512):
    t, d = h2.shape
    e = N_EXPERTS
    kk = TOPK_EXPERTS
    tok = lambda: pl.BlockSpec((kk, tm), lambda i: (0, i))
    return pl.pallas_call(
        _route_kernel,
        out_shape=(jax.ShapeDtypeStruct((kk, t), I32), jax.ShapeDtypeStruct((kk, t), F32),
                   jax.ShapeDtypeStruct((kk, t), I32), jax.ShapeDtypeStruct((e, 1), I32)),
        grid=(t // tm,),
        in_specs=[pl.BlockSpec((tm, d), lambda i: (i, 0)),
                  pl.BlockSpec((e, d), lambda i: (0, 0)),
                  pl.BlockSpec((e, 1), lambda i: (0, 0))],
        out_specs=(tok(), tok(), tok(), pl.BlockSpec((e, 1), lambda i: (0, 0))),
        scratch_shapes=[pltpu.VMEM((e, 1), F32)],
        compiler_params=_cparams(1),
        name="route",
    )(h2, w_router_t, router_bias.reshape(e, 1))


def _dest_kernel(eidx_ref, slot_ref, pstart_ref, o_ref):
    kk, tm = eidx_ref.shape
    nsub, _, tb = o_ref.shape
    eiota = lax.broadcasted_iota(I32, (N_EXPERTS, tm), 0)
    rows = []
    for k in range(kk):
        base = jnp.sum(jnp.where(eiota == eidx_ref[k:k + 1, :], pstart_ref[...], 0), axis=0, keepdims=True)
        rows.append(base + slot_ref[k:k + 1, :])
    dest = jnp.concatenate(rows, axis=0)
    for s in range(nsub):
        o_ref[s] = dest[:, s * tb:(s + 1) * tb]


def _dest(eidx, slot, pstart, tb, tm=2048):
    kk, t = eidx.shape
    tm = min(tm, t)
    return pl.pallas_call(
        _dest_kernel,
        out_shape=jax.ShapeDtypeStruct((t // tb, kk, tb), I32),
        grid=(t // tm,),
        in_specs=[pl.BlockSpec((kk, tm), lambda i: (0, i)),
                  pl.BlockSpec((kk, tm), lambda i: (0, i)),
                  pl.BlockSpec((N_EXPERTS, 1), lambda i: (0, 0))],
        out_specs=pl.BlockSpec((tm // tb, kk, tb), lambda i: (i, 0, 0)),
        compiler_params=_cparams(1),
        name="dest",
    )(eidx, slot, pstart.reshape(N_EXPERTS, 1))


def _pack_halves(x):
    n = x.shape[1] // 2
    hi = lax.bitcast_convert_type(x[:, :n].astype(BF16).astype(F32), U32)
    lo = lax.bitcast_convert_type(x[:, n:].astype(BF16).astype(F32), U32)
    return hi | (lo >> 16)


def _unpack_halves(p):
    hi = lax.bitcast_convert_type(p & jnp.uint32(0xFFFF0000), F32)
    lo = lax.bitcast_convert_type(p << 16, F32)
    return hi, lo


def _dispatch_kernel(last_ref, has_ref, dest_hbm, h_ref, xs_hbm, dest_smem, zero_ref, sem_idx, sem_row, sem_zero):
    tb = h_ref.shape[0]
    rb = zero_ref.shape[0]
    step = pl.program_id(0)

    @pl.when(step == 0)
    def _():
        zero_ref[...] = jnp.zeros_like(zero_ref)

        def zero_copy(e):
            return pltpu.make_async_copy(zero_ref, xs_hbm.at[pl.ds(pl.multiple_of(last_ref[e], rb), rb), :],
                                         sem_zero)

        def zero_start(e, carry):
            @pl.when(has_ref[e] > 0)
            def _():
                zero_copy(e).start()
            return carry

        def zero_wait(e, carry):
            @pl.when(has_ref[e] > 0)
            def _():
                zero_copy(e).wait()
            return carry

        lax.fori_loop(0, N_EXPERTS, zero_start, 0)
        lax.fori_loop(0, N_EXPERTS, zero_wait, 0)

    idx_copy = pltpu.make_async_copy(dest_hbm.at[step], dest_smem, sem_idx)
    idx_copy.start()
    idx_copy.wait()

    def row_copy(k, r):
        return pltpu.make_async_copy(h_ref.at[pl.ds(r, 1), :],
                                     xs_hbm.at[pl.ds(dest_smem[k, r], 1), :], sem_row)

    def issue(r, carry):
        for k in range(TOPK_EXPERTS):
            row_copy(k, r).start()
        return carry

    def drain(r, carry):
        for k in range(TOPK_EXPERTS):
            row_copy(k, r).wait()
        return carry

    lax.fori_loop(0, tb, issue, 0)
    lax.fori_loop(0, tb, drain, 0)


def _dispatch(last_start, has_rows, dest_blocks, h2p, n_rows):
    t, w = h2p.shape
    nsteps, kk, tb = dest_blocks.shape
    rb = ROW_BLOCK
    return pl.pallas_call(
        _dispatch_kernel,
        out_shape=jax.ShapeDtypeStruct((n_rows, w), U32),
        grid_spec=pltpu.PrefetchScalarGridSpec(
            num_scalar_prefetch=2,
            grid=(nsteps,),
            in_specs=[pl.BlockSpec(memory_space=pl.ANY),
                      pl.BlockSpec((tb, w), lambda i, ls, hr: (i, 0))],
            out_specs=pl.BlockSpec(memory_space=pl.ANY),
            scratch_shapes=[pltpu.SMEM((kk, tb), I32), pltpu.VMEM((rb, w), U32),
                            pltpu.SemaphoreType.DMA, pltpu.SemaphoreType.DMA, pltpu.SemaphoreType.DMA],
        ),
        compiler_params=_cparams(1),
        name="dispatch",
    )(last_start, has_rows, dest_blocks, h2p)


def _ffn_kernel(be_ref, nused_ref, xs_ref, wg_ref, wu_ref, wd_ref, ys_ref, wgu_ref, wdn_ref):
    i = pl.program_id(0)
    ff = wg_ref.shape[-1]
    half = xs_ref.shape[-1]

    @pl.when((i == 0) | (be_ref[i] != be_ref[jnp.maximum(i - 1, 0)]))
    def _():
        wgu_ref[:, :ff] = wg_ref[0].astype(BF16)
        wgu_ref[:, ff:] = wu_ref[0].astype(BF16)
        wdn_ref[...] = wd_ref[0].astype(BF16)

    @pl.when(i < nused_ref[0])
    def _():
        x_hi, x_lo = _unpack_halves(xs_ref[...])
        au = _dot(x_hi.astype(BF16), wgu_ref[:half, :]) + _dot(x_lo.astype(BF16), wgu_ref[half:, :])
        a, u = au[:, :ff], au[:, ff:]
        act = (a * _sigmoid(a) * u).astype(BF16)
        ys_ref[...] = _pack_halves(_dot(act, wdn_ref[...]))

    @pl.when(i >= nused_ref[0])
    def _():
        ys_ref[...] = jnp.zeros_like(ys_ref)


def _ffn(blk_expert, n_used, xs, wg, wu, wd):
    n_rows, w = xs.shape
    rb = ROW_BLOCK
    _, d, ff = wg.shape
    return pl.pallas_call(
        _ffn_kernel,
        out_shape=jax.ShapeDtypeStruct((n_rows, w), U32),
        grid_spec=pltpu.PrefetchScalarGridSpec(
            num_scalar_prefetch=2,
            grid=(n_rows // rb,),
            in_specs=[pl.BlockSpec((rb, w), lambda i, be, nu: (i, 0)),
                      pl.BlockSpec((1, d, ff), lambda i, be, nu: (be[i], 0, 0)),
                      pl.BlockSpec((1, d, ff), lambda i, be, nu: (be[i], 0, 0)),
                      pl.BlockSpec((1, ff, d), lambda i, be, nu: (be[i], 0, 0))],
            out_specs=pl.BlockSpec((rb, w), lambda i, be, nu: (i, 0)),
            scratch_shapes=[pltpu.VMEM((d, 2 * ff), BF16), pltpu.VMEM((ff, d), BF16)],
        ),
        compiler_params=_cparams(1),
        name="ffn",
    )(blk_expert, n_used, xs, wg, wu, wd)


def _combine_kernel(dest_hbm, ys_hbm, wts_ref, h_ref, x1_ref, gate2_ref, fg_ref, wgus_ref, wds_ref,
                    o_ref, dest_smem, rows_ref, sem_idx, sem_row):
    tb = h_ref.shape[1]
    step = pl.program_id(0) * pl.num_programs(1) + pl.program_id(1)
    nsteps = pl.num_programs(0) * pl.num_programs(1)
    slot = step % 2

    def gather_rows(s, into):
        idx_copy = pltpu.make_async_copy(dest_hbm.at[s], dest_smem.at[into], sem_idx)
        idx_copy.start()
        idx_copy.wait()

        def issue(r, carry):
            for k in range(TOPK_EXPERTS):
                pltpu.make_async_copy(ys_hbm.at[pl.ds(dest_smem[into, k, r], 1), :],
                                      rows_ref.at[into, k, pl.ds(r, 1), :], sem_row.at[into]).start()
            return carry

        lax.fori_loop(0, tb, issue, 0)

    @pl.when(step == 0)
    def _():
        gather_rows(step, slot)

    @pl.when(step + 1 < nsteps)
    def _():
        gather_rows(step + 1, 1 - slot)

    def drain(r, carry):
        for k in range(TOPK_EXPERTS):
            pltpu.make_async_copy(ys_hbm.at[pl.ds(0, 1), :], rows_ref.at[slot, k, pl.ds(r, 1), :],
                                  sem_row.at[slot]).wait()
        return carry

    half = h_ref.shape[-1]
    ff = wds_ref.shape[0]
    h_hi, h_lo = _unpack_halves(h_ref[0])
    au = _dot(h_hi.astype(BF16), wgus_ref[:half, :]) + _dot(h_lo.astype(BF16), wgus_ref[half:, :])
    a, u = au[:, :ff], au[:, ff:]
    y = _dot((a * _sigmoid(a) * u).astype(BF16), wds_ref[...])

    lax.fori_loop(0, tb, drain, 0)
    w = wts_ref[0]
    y_hi, y_lo = y[:, :half], y[:, half:]
    for k in range(TOPK_EXPERTS):
        r_hi, r_lo = _unpack_halves(rows_ref[slot, k])
        y_hi = y_hi + w[:, k:k + 1] * r_hi
        y_lo = y_lo + w[:, k:k + 1] * r_lo
    y = jnp.concatenate([y_hi, y_lo], axis=1)
    x2 = x1_ref[0] + gate2_ref[0] * y
    o_ref[0] = x2 * lax.rsqrt(jnp.mean(x2 * x2, axis=-1, keepdims=True) + EPS) * fg_ref[...]


def _combine(dest_blocks, ys, wts, h2p, x1, gate2, final_g, wgus, wds):
    b, s, d = x1.shape
    nsteps, kk, tb = dest_blocks.shape
    w = h2p.shape[-1]
    per_b = s // tb
    row = lambda n: pl.BlockSpec((1, tb, n), lambda bi, i: (bi, i, 0))
    full = lambda a: pl.BlockSpec(a.shape, lambda bi, i: (0,) * a.ndim)
    return pl.pallas_call(
        _combine_kernel,
        out_shape=jax.ShapeDtypeStruct((b, s, d), F32),
        grid=(b, per_b),
        in_specs=[pl.BlockSpec(memory_space=pl.ANY),
                  pl.BlockSpec(memory_space=pl.ANY),
                  row(kk), row(w), row(d),
                  pl.BlockSpec((1, 1, d), lambda bi, i: (bi, 0, 0)),
                  full(final_g), full(wgus), full(wds)],
        out_specs=row(d),
        scratch_shapes=[pltpu.SMEM((2, kk, tb), I32), pltpu.VMEM((2, kk, tb, w), U32),
                        pltpu.SemaphoreType.DMA, pltpu.SemaphoreType.DMA((2,))],
        compiler_params=_cparams(2),
        name="combine",
    )(dest_blocks, ys, wts, h2p, x1, gate2, final_g, wgus, wds)


def _layer(x, mod, norm1_g, norm2_g, w_in, w_branch_sb, w_branch_mb, w_out, w_router, router_bias,
           w_gate_e, w_up_e, w_down_e, w_gate_sh, w_up_sh, w_down_sh, bias_tab, final_g):
    b, s, d = x.shape
    t = b * s
    nqkv = 3 * SB_WIDTH + 3 * MB_WIDTH

    col = jnp.arange(nqkv)
    is_q = (col < SB_WIDTH) | ((col >= 3 * SB_WIDTH) & (col < 3 * SB_WIDTH + MB_WIDTH))
    wqkv = (w_in[:, :nqkv] * jnp.where(is_q, HEAD_DIM ** -0.5, 1.0)).astype(BF16)
    wg = w_in[:, nqkv:].astype(BF16)
    qkv, gates = _in_proj(x, mod, norm1_g.reshape(1, d), wqkv, wg)

    cb = SB_WIDTH // LANES
    o_sb = _sb_attn(qkv, 0, cb, 2 * cb)
    o_mb = _mb_attn(qkv, bias_tab, 3 * cb, 4 * cb, 5 * cb)

    x1, h2, h2p = _out_proj(o_sb, o_mb, gates, x, mod, norm2_g.reshape(1, d), w_branch_sb.astype(BF16),
                            w_branch_mb.astype(BF16), w_out.astype(BF16))

    eidx, wts, slot, counts = _route(h2.reshape(t, d), w_router.T, router_bias)

    rb = ROW_BLOCK
    counts = counts.reshape(N_EXPERTS)
    pcounts = (counts + rb - 1) // rb * rb
    pend = jnp.cumsum(pcounts)
    pstart = pend - pcounts
    n_blk = t * TOPK_EXPERTS // rb + N_EXPERTS
    blk_row = jnp.arange(n_blk, dtype=I32) * rb
    blk_expert = jnp.minimum(jnp.sum(pend[None, :] <= blk_row[:, None], axis=1), N_EXPERTS - 1).astype(I32)
    n_used = (pend[-1:] // rb).astype(I32)
    tb = 256
    dest_blocks = _dest(eidx, slot, pstart.astype(I32), tb)

    xs = _dispatch((pend - rb).astype(I32), pcounts.astype(I32), dest_blocks, h2p.reshape(t, d // 2), n_blk * rb)
    ys = _ffn(blk_expert, n_used, xs, w_gate_e, w_up_e, w_down_e)
    wts_tok = wts.T.reshape(b, s, TOPK_EXPERTS)
    wgus = jnp.concatenate([w_gate_sh, w_up_sh], axis=1).astype(BF16)
    return _combine(dest_blocks, ys, wts_tok, h2p, x1, mod[:, 5:6, :], final_g.reshape(1, d), wgus,
                    w_down_sh.astype(BF16))


def kernel(x, c, norm1_g, norm2_g, w_ada, b_ada, w_in, w_branch_sb, w_branch_mb, w_out, w_router, router_bias,
           w_gate_e, w_up_e, w_down_e, w_gate_sh, w_up_sh, w_down_sh, rel_bias, final_g):
    b, s, d = x.shape
    depth = w_ada.shape[0]
    assert depth == 1, "the final norm is fused into the only layer's combine step"
    bias_tab = _moba_bias_table(rel_bias)
    l = 0
    mod = _ada(c, w_ada[l], b_ada[l]).reshape(b, N_MOD, d)
    return _layer(x, mod, norm1_g[l], norm2_g[l], w_in[l], w_branch_sb[l], w_branch_mb[l], w_out[l],
                  w_router[l], router_bias[l], w_gate_e[l], w_up_e[l], w_down_e[l], w_gate_sh[l],
                  w_up_sh[l], w_down_sh[l], bias_tab, final_g)
```

```python
import functools

import jax
import jax.numpy as jnp
from jax import lax
from jax.experimental import pallas as pl
from jax.experimental.pallas import tpu as pltpu
from jax.experimental.pallas import tpu_sc as plsc

F32 = jnp.float32
BF16 = jnp.bfloat16
I32 = jnp.int32
U32 = jnp.uint32

HEAD_DIM = 64
SB_HEADS = 8
MB_HEADS = 8
SB_WIDTH = SB_HEADS * HEAD_DIM
MB_WIDTH = MB_HEADS * HEAD_DIM
MB_BLOCK = 256
MB_TOPK = 3
REL_BUCKETS = 32
REL_MAX_DIST = 128
N_EXPERTS = 64
N_GROUPS = 8
GROUP_SIZE = N_EXPERTS // N_GROUPS
TOPK_GROUPS = 4
TOPK_EXPERTS = 8
ROUTED_SCALE = 2.5
N_MOD = 6
EPS = 1e-6
NEG = -1e30

LANES = 128
HEAD_PAIR = LANES // HEAD_DIM
ATT_TILE = 256
SB_CUTOFF = 120.0
ONES_ROWS = 16
ROW_BLOCK = 512

SC_CORES = 2
SC_SUBCORES = 16
SC_WINDOW = 64
VMEM_LIMIT = 56 * 1024 * 1024


def _cparams(n_axes):
    return pltpu.CompilerParams(dimension_semantics=("arbitrary",) * n_axes,
                                vmem_limit_bytes=VMEM_LIMIT)


def _dot(a, b):
    return jnp.dot(a, b, preferred_element_type=F32)


def _dot_nt(a, b):
    return lax.dot_general(a, b, (((1,), (1,)), ((), ())), preferred_element_type=F32)


def _split3(a):
    p0 = a.astype(BF16)
    r0 = a - p0.astype(F32)
    p1 = r0.astype(BF16)
    p2 = (r0 - p1.astype(F32)).astype(BF16)
    return p0, p1, p2


def _sigmoid(x):
    return 1.0 / (1.0 + jnp.exp(-x))


def _rms_mod(x, g, shift, scale):
    y = x * lax.rsqrt(jnp.mean(x * x, axis=-1, keepdims=True) + EPS) * g
    return y * (1.0 + scale) + shift


def _ada_kernel(c_ref, w_ref, b_ref, o_ref):
    c = c_ref[...]
    s = c * _sigmoid(c)
    s0, s1, _ = _split3(s)
    w0, w1, _ = _split3(w_ref[...])
    o_ref[...] = _dot(s0, w0) + _dot(s0, w1) + _dot(s1, w0) + b_ref[...]


def _ada(c, w_ada, b_ada):
    b, d = c.shape
    n = w_ada.shape[1]
    tn = 1536
    return pl.pallas_call(
        _ada_kernel,
        out_shape=jax.ShapeDtypeStruct((b, n), F32),
        grid=(n // tn,),
        in_specs=[pl.BlockSpec((b, d), lambda j: (0, 0)),
                  pl.BlockSpec((d, tn), lambda j: (0, j)),
                  pl.BlockSpec((1, tn), lambda j: (0, j))],
        out_specs=pl.BlockSpec((b, tn), lambda j: (0, j)),
        compiler_params=_cparams(1),
        name="ada",
    )(c, w_ada, b_ada.reshape(1, n))


def _in_proj_kernel(x_ref, mod_ref, g_ref, wqkv_ref, wg_ref, qkv_ref, gate_ref):
    h = _rms_mod(x_ref[0], g_ref[...], mod_ref[0, 0:1, :], mod_ref[0, 1:2, :]).astype(BF16)
    qkv_ref[0] = _dot(h, wqkv_ref[...]).astype(BF16)
    gate_ref[0] = _sigmoid(_dot(h, wg_ref[...])).astype(BF16)


def _in_proj(x, mod, g, wqkv, wg, tm=512):
    b, s, d = x.shape
    nq, ng = wqkv.shape[1], wg.shape[1]
    return pl.pallas_call(
        _in_proj_kernel,
        out_shape=(jax.ShapeDtypeStruct((b, s, nq), BF16), jax.ShapeDtypeStruct((b, s, ng), BF16)),
        grid=(b, s // tm),
        in_specs=[pl.BlockSpec((1, tm, d), lambda bi, i: (bi, i, 0)),
                  pl.BlockSpec((1, N_MOD, d), lambda bi, i: (bi, 0, 0)),
                  pl.BlockSpec((1, d), lambda bi, i: (0, 0)),
                  pl.BlockSpec((d, nq), lambda bi, i: (0, 0)),
                  pl.BlockSpec((d, ng), lambda bi, i: (0, 0))],
        out_specs=(pl.BlockSpec((1, tm, nq), lambda bi, i: (bi, i, 0)),
                   pl.BlockSpec((1, tm, ng), lambda bi, i: (bi, i, 0))),
        compiler_params=_cparams(2),
        name="in_proj",
    )(x, mod, g, wqkv, wg)


def _head_mask(h):
    lane = lax.broadcasted_iota(I32, (ATT_TILE, LANES), 1)
    return (lane >= h * HEAD_DIM) & (lane < (h + 1) * HEAD_DIM)


def _sb_kernel(q_ref, k_ref, v_ref, o_ref):
    t = ATT_TILE
    rows = HEAD_PAIR * t
    i = pl.program_id(2)
    row = lax.broadcasted_iota(I32, (t, t), 0)
    col = lax.broadcasted_iota(I32, (t, t), 1)
    later = jnp.where(row > col, 1.0, 0.0).astype(BF16)
    past = (lax.broadcasted_iota(I32, (rows, t), 1)
            < lax.broadcasted_iota(I32, (rows, t), 0) % t)
    q = q_ref[0]
    q2 = jnp.concatenate([jnp.where(_head_mask(h), q, jnp.zeros_like(q)) for h in range(HEAD_PAIR)], axis=0)

    def tile(j, carry, acc, diag):
        start = pl.multiple_of(j * t, t)
        kj = k_ref[0, pl.ds(start, t), :]
        vj = v_ref[0, pl.ds(start, t), :]
        z = _dot_nt(q2, kj)
        sp = jnp.maximum(z, 0.0) + jnp.log(1.0 + jnp.exp(-jnp.abs(z)))
        log_rest = -sp
        if diag:
            log_rest = jnp.where(past, log_rest, 0.0)
        hi = log_rest.astype(BF16)
        lo = (log_rest - hi.astype(F32)).astype(BF16)
        cum = _dot(jnp.concatenate([hi, lo], axis=0), later)
        a = jnp.exp(z - sp + cum[:rows] + cum[rows:] + carry)
        if diag:
            a = jnp.where(past, a, 0.0)
        return carry + jnp.sum(log_rest, axis=1, keepdims=True), acc + _dot(a.astype(BF16), vj)

    carry, acc = tile(i, jnp.zeros((rows, 1), F32), jnp.zeros((rows, LANES), F32), True)

    def live(st):
        j, carry, _ = st
        return (j >= 0) & (jnp.max(carry) > -SB_CUTOFF)

    def older(st):
        j, carry, acc = st
        carry, acc = tile(j, carry, acc, False)
        return j - 1, carry, acc

    _, _, acc = lax.while_loop(live, older, (i - 1, carry, acc))
    o_ref[0] = jnp.where(_head_mask(0), acc[:t], acc[t:]).astype(BF16)


def _sb_attn(qkv, q_col, k_col, v_col):
    b, s, _ = qkv.shape
    t = ATT_TILE
    npair = SB_HEADS // HEAD_PAIR
    return pl.pallas_call(
        _sb_kernel,
        out_shape=jax.ShapeDtypeStruct((b, s, SB_WIDTH), BF16),
        grid=(b, npair, s // t),
        in_specs=[pl.BlockSpec((1, t, LANES), lambda bi, p, i: (bi, i, q_col + p)),
                  pl.BlockSpec((1, s, LANES), lambda bi, p, i: (bi, 0, k_col + p)),
                  pl.BlockSpec((1, s, LANES), lambda bi, p, i: (bi, 0, v_col + p))],
        out_specs=pl.BlockSpec((1, t, LANES), lambda bi, p, i: (bi, i, p)),
        compiler_params=_cparams(3),
        name="sb_attn",
    )(qkv, qkv, qkv)


def _mb_kernel(qt_ref, k_ref, vt_ref, bias_ref, o_ref, kbar_ref, *, nb):
    t = ATT_TILE
    hd = HEAD_DIM
    i = pl.program_id(2)
    lane_row = lax.broadcasted_iota(I32, (1, LANES), 1)
    slot = lax.broadcasted_iota(I32, (LANES, t), 0)
    blk = lax.broadcasted_iota(I32, (hd, t), 0)

    @pl.when(i == 0)
    def _():
        kbar_ref[...] = jnp.zeros_like(kbar_ref)
        for n in range(nb):
            kbar_ref[n:n + 1, :] = jnp.mean(k_ref[0, n * t:(n + 1) * t, :].astype(F32), axis=0, keepdims=True)

    qt = qt_ref[0, 0]
    kb0, kb1, kb2 = _split3(kbar_ref[...])
    q_heads, not_sels = [], []
    for h in range(HEAD_PAIR):
        qh = jnp.where((slot >= h * hd) & (slot < (h + 1) * hd), qt, jnp.zeros_like(qt))
        gate = _dot(kb0, qh) + _dot(kb1, qh) + _dot(kb2, qh)
        gate = jnp.where(blk < i, gate, -jnp.inf)
        not_sel = jnp.ones((hd, t), F32)
        for _ in range(MB_TOPK):
            m = jnp.max(gate, axis=0, keepdims=True)
            idx = jnp.min(jnp.where(gate == m, blk, hd), axis=0, keepdims=True)
            pick = (blk == idx) & (m > -jnp.inf)
            not_sel = jnp.where(pick, 0.0, not_sel)
            gate = jnp.where(pick, -jnp.inf, gate)
        q_heads.append(qh)
        not_sels.append(not_sel.astype(BF16))
    q_ext = jnp.concatenate([jnp.concatenate(q_heads, axis=1), jnp.concatenate(not_sels, axis=1),
                             jnp.ones((ONES_ROWS, HEAD_PAIR * t), BF16),
                             jnp.zeros((hd - ONES_ROWS, HEAD_PAIR * t), BF16)], axis=0)
    ones_rows = jnp.ones((ONES_ROWS, t), BF16)

    def scores(j, masked, dead=None):
        kj = k_ref[0, pl.ds(pl.multiple_of(j * t, t), t), :]
        pen = jnp.where(lane_row == j, NEG, 0.0) if masked else jnp.zeros((1, LANES), F32)
        if dead is not None:
            pen = pen + jnp.where(lane_row == hd, jnp.where(dead, NEG, 0.0), 0.0)
        return _dot(jnp.concatenate([kj, jnp.broadcast_to(pen.astype(BF16), (t, LANES))], axis=1), q_ext)

    def weighted(j, p):
        return _dot(jnp.concatenate([vt_ref[0, j], ones_rows], axis=0), p)

    def probs(s, m_run):
        m_new = jnp.maximum(m_run, jnp.max(s, axis=0, keepdims=True))
        return m_new, jnp.exp(s - m_new).astype(BF16)

    i_prev = jnp.maximum(i - 1, 0)
    s_a = scores(i, False) + bias_ref[0, 0]
    s_b = scores(i_prev, True) + bias_ref[0, 1]
    m_a, p_a = probs(s_a, jnp.full((1, HEAD_PAIR * t), NEG, F32))
    m_b, p_b = probs(s_b, jnp.full((1, HEAD_PAIR * t), NEG, F32))
    acc_a = weighted(i, p_a)
    acc_b = weighted(i_prev, p_b)

    def older(n, c):
        m_a, acc_a, m_b, acc_b = c
        j_a = 2 * n
        j_b = 2 * n + 1
        dead = j_b >= i - 1
        j_b = jnp.minimum(j_b, jnp.maximum(i - 2, 0))
        s_a = scores(j_a, True)
        s_b = scores(j_b, True, dead)
        m_a2, p_a = probs(s_a, m_a)
        m_b2, p_b = probs(s_b, m_b)
        acc_a = jnp.exp(m_a - m_a2) * acc_a + weighted(j_a, p_a)
        acc_b = jnp.exp(m_b - m_b2) * acc_b + weighted(j_b, p_b)
        return m_a2, acc_a, m_b2, acc_b

    m_a, acc_a, m_b, acc_b = lax.fori_loop(0, i // 2, older, (m_a, acc_a, m_b, acc_b))
    m_run = jnp.maximum(m_a, m_b)
    acc = jnp.exp(m_a - m_run) * acc_a + jnp.exp(m_b - m_run) * acc_b
    denom = acc[LANES:LANES + 1, :]
    out_t = jnp.concatenate([acc[h * hd:(h + 1) * hd, h * t:(h + 1) * t] / denom[:, h * t:(h + 1) * t]
                             for h in range(HEAD_PAIR)], axis=0)
    o_ref[0] = out_t.T.astype(BF16)


def _bias_tab_kernel(rb_ref, bucket_ref, o_ref):
    h = pl.program_id(0)
    t = ATT_TILE
    far = rb_ref[h, REL_BUCKETS - 1]
    causal = lax.broadcasted_iota(I32, (t, t), 1) >= lax.broadcasted_iota(I32, (t, t), 0)
    for w in range(2):
        bk = bucket_ref[w]
        tile = jnp.zeros((t, t), F32)
        for b in range(REL_BUCKETS):
            tile = jnp.where(bk == b, rb_ref[h, b] - far, tile)
        o_ref[0, w] = jnp.where(causal, tile, NEG) if w == 0 else tile


def _moba_bias_table(rel_bias):
    assert MB_BLOCK >= REL_MAX_DIST and ATT_TILE == MB_BLOCK
    t = ATT_TILE
    nh = rel_bias.shape[0]
    d = jnp.arange(t, dtype=I32)[None, :] - jnp.arange(t, dtype=I32)[:, None]
    bucket = jnp.stack([_t5_bucket(d), _t5_bucket(d + t)]).astype(I32)
    return pl.pallas_call(
        _bias_tab_kernel,
        out_shape=jax.ShapeDtypeStruct((nh // HEAD_PAIR, 2, t, HEAD_PAIR * t), F32),
        grid=(nh,),
        in_specs=[pl.BlockSpec(memory_space=pltpu.SMEM),
                  pl.BlockSpec((2, t, t), lambda h: (0, 0, 0))],
        out_specs=pl.BlockSpec((1, 2, t, t), lambda h: (h // HEAD_PAIR, 0, 0, h % HEAD_PAIR)),
        compiler_params=_cparams(1),
        name="bias_tab",
    )(rel_bias, bucket)


def _blocked_t(a):
    b, s, w = a.shape
    return a.reshape(b, s // ATT_TILE, ATT_TILE, w).transpose(0, 1, 3, 2)


def _mb_attn(qkv, bias_tab, q_col, k_col, v_col):
    b, s, _ = qkv.shape
    t = ATT_TILE
    nb = s // t
    assert t == MB_BLOCK and s % t == 0 and nb <= HEAD_DIM and HEAD_PAIR == 2
    npair = MB_HEADS // HEAD_PAIR
    qt = _blocked_t(qkv[:, :, q_col * LANES:q_col * LANES + MB_WIDTH])
    vt = _blocked_t(qkv[:, :, v_col * LANES:v_col * LANES + MB_WIDTH])
    return pl.pallas_call(
        functools.partial(_mb_kernel, nb=nb),
        out_shape=jax.ShapeDtypeStruct((b, s, MB_WIDTH), BF16),
        grid=(b, npair, nb),
        in_specs=[pl.BlockSpec((1, 1, LANES, t), lambda bi, p, i: (bi, i, p, 0)),
                  pl.BlockSpec((1, s, LANES), lambda bi, p, i: (bi, 0, k_col + p)),
                  pl.BlockSpec((1, nb, LANES, t), lambda bi, p, i: (bi, 0, p, 0)),
                  pl.BlockSpec((1, 2, t, HEAD_PAIR * t), lambda bi, p, i: (p, 0, 0, 0))],
        out_specs=pl.BlockSpec((1, t, LANES), lambda bi, p, i: (bi, i, p)),
        scratch_shapes=[pltpu.VMEM((HEAD_DIM, LANES), F32)],
        compiler_params=_cparams(3),
        name="mb_attn",
    )(qt, qkv, vt, bias_tab)


def _t5_bucket(dist):
    n = jnp.maximum(dist, 0)
    max_exact = REL_BUCKETS // 2
    nf = jnp.maximum(n, 1).astype(F32)
    large = max_exact + (jnp.log(nf / max_exact) / jnp.log(jnp.float32(REL_MAX_DIST / max_exact))
                         * (REL_BUCKETS - max_exact)).astype(I32)
    large = jnp.minimum(large, REL_BUCKETS - 1)
    return jnp.where(n < max_exact, n, large)


def _out_proj_kernel(osb_ref, omb_ref, gate_ref, x_ref, mod_ref, g2_ref, wsb_ref, wmb_ref, wout_ref,
                     x1_ref, h2_ref, h2p_ref):
    d = x_ref.shape[-1]
    gates = gate_ref[0].astype(F32)
    merged = (gates[:, :d] * _dot(osb_ref[0], wsb_ref[...])
              + gates[:, d:] * _dot(omb_ref[0], wmb_ref[...]))
    x1 = x_ref[0] + mod_ref[0, 2:3, :] * _dot(merged.astype(BF16), wout_ref[...])
    x1_ref[0] = x1
    h2 = _rms_mod(x1, g2_ref[...], mod_ref[0, 3:4, :], mod_ref[0, 4:5, :])
    h2_ref[0] = h2
    h2p_ref[0] = _pack_halves(h2)


def _out_proj(osb, omb, gates, x, mod, g2, wsb, wmb, wout, tm=512):
    b, s, d = x.shape
    row = lambda w: pl.BlockSpec((1, tm, w), lambda bi, i: (bi, i, 0))
    full = lambda a: pl.BlockSpec(a.shape, lambda bi, i: (0,) * a.ndim)
    return pl.pallas_call(
        _out_proj_kernel,
        out_shape=(jax.ShapeDtypeStruct((b, s, d), F32), jax.ShapeDtypeStruct((b, s, d), F32),
                   jax.ShapeDtypeStruct((b, s, d // 2), U32)),
        grid=(b, s // tm),
        in_specs=[row(osb.shape[-1]), row(omb.shape[-1]), row(gates.shape[-1]), row(d),
                  pl.BlockSpec((1, N_MOD, d), lambda bi, i: (bi, 0, 0)),
                  full(g2), full(wsb), full(wmb), full(wout)],
        out_specs=(row(d), row(d), row(d // 2)),
        compiler_params=_cparams(2),
        name="out_proj",
    )(osb, omb, gates, x, mod, g2, wsb, wmb, wout)


def _first_argmax(v, iota, n):
    m = jnp.max(v, axis=0, keepdims=True)
    idx = jnp.min(jnp.where(v == m, iota, n), axis=0, keepdims=True)
    return m, idx


def _route_kernel(h_ref, wr_ref, rb_ref, eidx_ref, wts_ref, slot_ref, cnt_ref, carry_ref):
    tm = h_ref.shape[0]
    e, g, gs = N_EXPERTS, N_GROUPS, GROUP_SIZE
    step = pl.program_id(0)

    @pl.when(step == 0)
    def _():
        carry_ref[...] = jnp.zeros_like(carry_ref)

    h0, h1, _ = _split3(h_ref[...])
    w0, w1, _ = _split3(wr_ref[...])
    logits = _dot_nt(w0, h0) + _dot_nt(w0, h1) + _dot_nt(w1, h0)
    scores = _sigmoid(logits)
    choice = scores + rb_ref[...]

    c3 = choice.reshape(g, gs, tm)
    sub = lax.broadcasted_iota(I32, (g, gs, tm), 1)
    m1 = jnp.max(c3, axis=1, keepdims=True)
    i1 = jnp.min(jnp.where(c3 == m1, sub, gs), axis=1, keepdims=True)
    m2 = jnp.max(jnp.where(sub == i1, -jnp.inf, c3), axis=1, keepdims=True)
    gscore = (m1 + m2).reshape(g, tm)

    giota = lax.broadcasted_iota(I32, (g, tm), 0)
    gmask = jnp.zeros((g, tm), F32)
    for _ in range(TOPK_GROUPS):
        _, gi = _first_argmax(gscore, giota, g)
        pick = giota == gi
        gmask = jnp.where(pick, 1.0, gmask)
        gscore = jnp.where(pick, -jnp.inf, gscore)
    emask = jnp.broadcast_to(gmask.reshape(g, 1, tm), (g, gs, tm)).reshape(e, tm)
    masked = jnp.where(emask > 0.5, choice, NEG)

    eiota = lax.broadcasted_iota(I32, (e, tm), 0)
    chosen = jnp.zeros((e, tm), F32)
    idxs, ws = [], []
    for _ in range(TOPK_EXPERTS):
        _, ei = _first_argmax(masked, eiota, e)
        pick = eiota == ei
        idxs.append(ei)
        ws.append(jnp.sum(jnp.where(pick, scores, 0.0), axis=0, keepdims=True))
        chosen = jnp.where(pick, 1.0, chosen)
        masked = jnp.where(pick, -jnp.inf, masked)
    wsum = ws[0]
    for w in ws[1:]:
        wsum = wsum + w

    r = lax.broadcasted_iota(I32, (tm, tm), 0)
    c = lax.broadcasted_iota(I32, (tm, tm), 1)
    before = jnp.where(r < c, 1.0, 0.0).astype(BF16)
    prefix = _dot(chosen.astype(BF16), before) + carry_ref[...]
    for k in range(TOPK_EXPERTS):
        eidx_ref[k:k + 1, :] = idxs[k]
        wts_ref[k:k + 1, :] = ws[k] / wsum * ROUTED_SCALE
        slot = jnp.sum(jnp.where(eiota == idxs[k], prefix, 0.0), axis=0, keepdims=True)
        slot_ref[k:k + 1, :] = slot.astype(I32)
    carry_ref[...] = carry_ref[...] + jnp.sum(chosen, axis=1, keepdims=True)
    cnt_ref[...] = carry_ref[...].astype(I32)


def _route(h2, w_router_t, router_bias, tm=512):
    t, d = h2.shape
    e = N_EXPERTS
    kk = TOPK_EXPERTS
    tok = lambda: pl.BlockSpec((kk, tm), lambda i: (0, i))
    return pl.pallas_call(
        _route_kernel,
        out_shape=(jax.ShapeDtypeStruct((kk, t), I32), jax.ShapeDtypeStruct((kk, t), F32),
                   jax.ShapeDtypeStruct((kk, t), I32), jax.ShapeDtypeStruct((e, 1), I32)),
        grid=(t // tm,),
        in_specs=[pl.BlockSpec((tm, d), lambda i: (i, 0)),
                  pl.BlockSpec((e, d), lambda i: (0, 0)),
                  pl.BlockSpec((e, 1), lambda i: (0, 0))],
        out_specs=(tok(), tok(), tok(), pl.BlockSpec((e, 1), lambda i: (0, 0))),
        scratch_shapes=[pltpu.VMEM((e, 1), F32)],
        compiler_params=_cparams(1),
        name="route",
    )(h2, w_router_t, router_bias.reshape(e, 1))


def _dest_kernel(eidx_ref, slot_ref, pstart_ref, o_ref):
    kk, tm = eidx_ref.shape
    eiota = lax.broadcasted_iota(I32, (N_EXPERTS, tm), 0)
    for k in range(kk):
        base = jnp.sum(jnp.where(eiota == eidx_ref[k:k + 1, :], pstart_ref[...], 0), axis=0, keepdims=True)
        o_ref[k:k + 1, :] = base + slot_ref[k:k + 1, :]


def _dest(eidx, slot, pstart, tm=2048):
    kk, t = eidx.shape
    tm = min(tm, t)
    tok = pl.BlockSpec((kk, tm), lambda i: (0, i))
    return pl.pallas_call(
        _dest_kernel,
        out_shape=jax.ShapeDtypeStruct((kk, t), I32),
        grid=(t // tm,),
        in_specs=[tok, tok, pl.BlockSpec((N_EXPERTS, 1), lambda i: (0, 0))],
        out_specs=tok,
        compiler_params=_cparams(1),
        name="dest",
    )(eidx, slot, pstart.reshape(N_EXPERTS, 1))


def _pack_halves(x):
    n = x.shape[1] // 2
    hi = lax.bitcast_convert_type(x[:, :n].astype(BF16).astype(F32), U32)
    lo = lax.bitcast_convert_type(x[:, n:].astype(BF16).astype(F32), U32)
    return hi | (lo >> 16)


def _unpack_halves(p):
    hi = lax.bitcast_convert_type(p & jnp.uint32(0xFFFF0000), F32)
    lo = lax.bitcast_convert_type(p << 16, F32)
    return hi, lo


def _sc_gather_rows(table, idx):
    n = idx.shape[0]
    w = table.shape[1]
    workers = SC_CORES * SC_SUBCORES
    assert n % (workers * SC_WINDOW) == 0
    per_worker = n // workers
    mesh = plsc.VectorSubcoreMesh(core_axis_name="c", subcore_axis_name="s")

    @functools.partial(
        pl.kernel, mesh=mesh,
        out_type=jax.ShapeDtypeStruct((n, w), table.dtype),
        scratch_types=[pltpu.VMEM((SC_WINDOW,), I32), pltpu.VMEM((SC_WINDOW, w), table.dtype),
                       pltpu.SemaphoreType.DMA],
    )
    def gather(table_hbm, idx_hbm, out_hbm, idx_v, rows_v, sem):
        wid = lax.axis_index("s") * SC_CORES + lax.axis_index("c")

        @pl.loop(0, per_worker // SC_WINDOW)
        def _(g):
            base = wid * per_worker + g * SC_WINDOW
            pltpu.sync_copy(idx_hbm.at[pl.ds(base, SC_WINDOW)], idx_v)
            pltpu.async_copy(table_hbm.at[idx_v], rows_v, sem).wait()
            pltpu.sync_copy(rows_v, out_hbm.at[pl.ds(base, SC_WINDOW)])

    return gather(table, idx)


def _sc_scatter_rows(rows, idx, n_out):
    t, w = rows.shape
    kk = idx.shape[0]
    win = LANES
    workers = SC_CORES * SC_SUBCORES
    assert t % (workers * win) == 0
    per_worker = t // workers
    mesh = plsc.VectorSubcoreMesh(core_axis_name="c", subcore_axis_name="s")

    @functools.partial(
        pl.kernel, mesh=mesh,
        out_type=jax.ShapeDtypeStruct((n_out, w), rows.dtype),
        scratch_types=[pltpu.VMEM((kk, win), I32), pltpu.VMEM((win, w), rows.dtype),
                       pltpu.SemaphoreType.DMA],
    )
    def scatter(rows_hbm, idx_hbm, out_hbm, idx_v, rows_v, sem):
        wid = lax.axis_index("s") * SC_CORES + lax.axis_index("c")

        @pl.loop(0, per_worker // win)
        def _(g):
            base = wid * per_worker + g * win
            pltpu.sync_copy(idx_hbm.at[:, pl.ds(base, win)], idx_v)
            pltpu.sync_copy(rows_hbm.at[pl.ds(base, win)], rows_v)
            for k in range(kk):
                pltpu.async_copy(rows_v, out_hbm.at[idx_v.at[k]], sem).wait()

    return scatter(rows, idx)


def _ffn_kernel(be_ref, valid_ref, xs_ref, wg_ref, wu_ref, wd_ref, ys_ref, wgu_ref, wdn_ref):
    i = pl.program_id(0)
    ff = wg_ref.shape[-1]
    rb, half = xs_ref.shape

    @pl.when((i == 0) | (be_ref[i] != be_ref[jnp.maximum(i - 1, 0)]))
    def _():
        wgu_ref[:, :ff] = wg_ref[0].astype(BF16)
        wgu_ref[:, ff:] = wu_ref[0].astype(BF16)
        wdn_ref[...] = wd_ref[0].astype(BF16)

    @pl.when(valid_ref[i] > 0)
    def _():
        is_token = lax.broadcasted_iota(I32, (rb, half), 0) < valid_ref[i]
        x_hi, x_lo = _unpack_halves(jnp.where(is_token, xs_ref[...], jnp.zeros((rb, half), xs_ref.dtype)))
        au = _dot(x_hi.astype(BF16), wgu_ref[:half, :]) + _dot(x_lo.astype(BF16), wgu_ref[half:, :])
        a, u = au[:, :ff], au[:, ff:]
        act = (a * _sigmoid(a) * u).astype(BF16)
        ys_ref[...] = _pack_halves(_dot(act, wdn_ref[...]))

    @pl.when(valid_ref[i] <= 0)
    def _():
        ys_ref[...] = jnp.zeros_like(ys_ref)


def _ffn(blk_expert, blk_valid, xs, wg, wu, wd):
    n_rows, w = xs.shape
    rb = ROW_BLOCK
    _, d, ff = wg.shape
    return pl.pallas_call(
        _ffn_kernel,
        out_shape=jax.ShapeDtypeStruct((n_rows, w), U32),
        grid_spec=pltpu.PrefetchScalarGridSpec(
            num_scalar_prefetch=2,
            grid=(n_rows // rb,),
            in_specs=[pl.BlockSpec((rb, w), lambda i, be, bv: (i, 0)),
                      pl.BlockSpec((1, d, ff), lambda i, be, bv: (be[i], 0, 0)),
                      pl.BlockSpec((1, d, ff), lambda i, be, bv: (be[i], 0, 0)),
                      pl.BlockSpec((1, ff, d), lambda i, be, bv: (be[i], 0, 0))],
            out_specs=pl.BlockSpec((rb, w), lambda i, be, bv: (i, 0)),
            scratch_shapes=[pltpu.VMEM((d, 2 * ff), BF16), pltpu.VMEM((ff, d), BF16)],
        ),
        compiler_params=_cparams(1),
        name="ffn",
    )(blk_expert, blk_valid, xs, wg, wu, wd)


def _combine_kernel(rows_ref, wts_ref, h_ref, x1_ref, gate2_ref, fg_ref, wgus_ref, wds_ref, o_ref):
    half = h_ref.shape[-1]
    ff = wds_ref.shape[0]
    h_hi, h_lo = _unpack_halves(h_ref[0])
    au = _dot(h_hi.astype(BF16), wgus_ref[:half, :]) + _dot(h_lo.astype(BF16), wgus_ref[half:, :])
    a, u = au[:, :ff], au[:, ff:]
    y = _dot((a * _sigmoid(a) * u).astype(BF16), wds_ref[...])

    w = wts_ref[0]
    y_hi, y_lo = y[:, :half], y[:, half:]
    for k in range(TOPK_EXPERTS):
        r_hi, r_lo = _unpack_halves(rows_ref[k, 0])
        y_hi = y_hi + w[:, k:k + 1] * r_hi
        y_lo = y_lo + w[:, k:k + 1] * r_lo
    y = jnp.concatenate([y_hi, y_lo], axis=1)
    x2 = x1_ref[0] + gate2_ref[0] * y
    o_ref[0] = x2 * lax.rsqrt(jnp.mean(x2 * x2, axis=-1, keepdims=True) + EPS) * fg_ref[...]


def _combine(rows, wts, h2p, x1, gate2, final_g, wgus, wds, tb=256):
    b, s, d = x1.shape
    kk = rows.shape[0]
    w = h2p.shape[-1]
    row = lambda n: pl.BlockSpec((1, tb, n), lambda bi, i: (bi, i, 0))
    full = lambda a: pl.BlockSpec(a.shape, lambda bi, i: (0,) * a.ndim)
    return pl.pallas_call(
        _combine_kernel,
        out_shape=jax.ShapeDtypeStruct((b, s, d), F32),
        grid=(b, s // tb),
        in_specs=[pl.BlockSpec((kk, 1, tb, w), lambda bi, i: (0, bi, i, 0)),
                  row(kk), row(w), row(d),
                  pl.BlockSpec((1, 1, d), lambda bi, i: (bi, 0, 0)),
                  full(final_g), full(wgus), full(wds)],
        out_specs=row(d),
        compiler_params=_cparams(2),
        name="combine",
    )(rows, wts, h2p, x1, gate2, final_g, wgus, wds)


def _layer(x, mod, norm1_g, norm2_g, w_in, w_branch_sb, w_branch_mb, w_out, w_router, router_bias,
           w_gate_e, w_up_e, w_down_e, w_gate_sh, w_up_sh, w_down_sh, bias_tab, final_g):
    b, s, d = x.shape
    t = b * s
    nqkv = 3 * SB_WIDTH + 3 * MB_WIDTH

    col = jnp.arange(nqkv)
    is_q = (col < SB_WIDTH) | ((col >= 3 * SB_WIDTH) & (col < 3 * SB_WIDTH + MB_WIDTH))
    wqkv = (w_in[:, :nqkv] * jnp.where(is_q, HEAD_DIM ** -0.5, 1.0)).astype(BF16)
    wg = w_in[:, nqkv:].astype(BF16)
    qkv, gates = _in_proj(x, mod, norm1_g.reshape(1, d), wqkv, wg)

    cb = SB_WIDTH // LANES
    o_sb = _sb_attn(qkv, 0, cb, 2 * cb)
    o_mb = _mb_attn(qkv, bias_tab, 3 * cb, 4 * cb, 5 * cb)

    x1, h2, h2p = _out_proj(o_sb, o_mb, gates, x, mod, norm2_g.reshape(1, d), w_branch_sb.astype(BF16),
                            w_branch_mb.astype(BF16), w_out.astype(BF16))

    eidx, wts, slot, counts = _route(h2.reshape(t, d), w_router.T, router_bias)

    rb = ROW_BLOCK
    counts = counts.reshape(N_EXPERTS)
    pcounts = (counts + rb - 1) // rb * rb
    pend = jnp.cumsum(pcounts)
    pstart = pend - pcounts
    n_blk = t * TOPK_EXPERTS // rb + N_EXPERTS
    blk_row = jnp.arange(n_blk, dtype=I32) * rb
    blk_expert = jnp.minimum(jnp.sum(pend[None, :] <= blk_row[:, None], axis=1), N_EXPERTS - 1).astype(I32)
    blk_valid = jnp.clip((pstart + counts)[blk_expert] - blk_row, 0, rb).astype(I32)
    dest = _dest(eidx, slot, pstart.astype(I32))

    as_i32 = lambda a: lax.bitcast_convert_type(a, I32)
    xs = _sc_scatter_rows(as_i32(h2p.reshape(t, d // 2)), dest, n_blk * rb)
    ys = _ffn(blk_expert, blk_valid, lax.bitcast_convert_type(xs, U32), w_gate_e, w_up_e, w_down_e)
    rows = _sc_gather_rows(as_i32(ys), dest.reshape(TOPK_EXPERTS * t))
    rows = lax.bitcast_convert_type(rows, U32).reshape(TOPK_EXPERTS, b, s, d // 2)
    wts_tok = wts.T.reshape(b, s, TOPK_EXPERTS)
    wgus = jnp.concatenate([w_gate_sh, w_up_sh], axis=1).astype(BF16)
    return _combine(rows, wts_tok, h2p, x1, mod[:, 5:6, :], final_g.reshape(1, d), wgus, w_down_sh.astype(BF16))


def kernel(x, c, norm1_g, norm2_g, w_ada, b_ada, w_in, w_branch_sb, w_branch_mb, w_out, w_router, router_bias,
           w_gate_e, w_up_e, w_down_e, w_gate_sh, w_up_sh, w_down_sh, rel_bias, final_g):
    b, s, d = x.shape
    depth = w_ada.shape[0]
    assert depth == 1, "the final norm is fused into the only layer's combine step"
    bias_tab = _moba_bias_table(rel_bias)
    l = 0
    mod = _ada(c, w_ada[l], b_ada[l]).reshape(b, N_MOD, d)
    return _layer(x, mod, norm1_g[l], norm2_g[l], w_in[l], w_branch_sb[l], w_branch_mb[l], w_out[l],
                  w_router[l], router_bias[l], w_gate_e[l], w_up_e[l], w_down_e[l], w_gate_sh[l],
                  w_up_sh[l], w_down_sh[l], bias_tab, final_g)
```

```python
import functools

import jax
import jax.numpy as jnp
from jax import lax
from jax.experimental import pallas as pl
from jax.experimental.pallas import tpu as pltpu
from jax.experimental.pallas import tpu_sc as plsc

F32 = jnp.float32
BF16 = jnp.bfloat16
I32 = jnp.int32

HEAD_DIM = 64
SB_HEADS = 8
MB_HEADS = 8
SB_WIDTH = SB_HEADS * HEAD_DIM
MB_WIDTH = MB_HEADS * HEAD_DIM
MB_BLOCK = 256
MB_TOPK = 3
REL_BUCKETS = 32
REL_MAX_DIST = 128
N_EXPERTS = 64
N_GROUPS = 8
GROUP_SIZE = N_EXPERTS // N_GROUPS
TOPK_GROUPS = 4
TOPK_EXPERTS = 8
ROUTED_SCALE = 2.5
N_MOD = 6
EPS = 1e-6
NEG = -1e30

LANES = 128
HEAD_PAIR = LANES // HEAD_DIM
ATT_TILE = 256
SB_CUTOFF = 120.0
ONES_ROWS = 16
ROW_BLOCK = 512

SC_CORES = 2
SC_SUBCORES = 16
SC_WINDOW = 64
VMEM_LIMIT = 56 * 1024 * 1024


def _cparams(n_axes):
    return pltpu.CompilerParams(dimension_semantics=("arbitrary",) * n_axes,
                                vmem_limit_bytes=VMEM_LIMIT)


def _dot(a, b):
    return jnp.dot(a, b, preferred_element_type=F32)


def _dot_nt(a, b):
    return lax.dot_general(a, b, (((1,), (1,)), ((), ())), preferred_element_type=F32)


def _split3(a):
    p0 = a.astype(BF16)
    r0 = a - p0.astype(F32)
    p1 = r0.astype(BF16)
    p2 = (r0 - p1.astype(F32)).astype(BF16)
    return p0, p1, p2


def _sigmoid(x):
    return 1.0 / (1.0 + jnp.exp(-x))


def _rms_mod(x, g, shift, scale):
    y = x * lax.rsqrt(jnp.mean(x * x, axis=-1, keepdims=True) + EPS) * g
    return y * (1.0 + scale) + shift


def _ada_kernel(c_ref, w_ref, b_ref, o_ref):
    c = c_ref[...]
    s = c * _sigmoid(c)
    s0, s1, _ = _split3(s)
    w0, w1, _ = _split3(w_ref[...])
    o_ref[...] = _dot(s0, w0) + _dot(s0, w1) + _dot(s1, w0) + b_ref[...]


def _ada(c, w_ada, b_ada):
    b, d = c.shape
    n = w_ada.shape[1]
    tn = 1536
    return pl.pallas_call(
        _ada_kernel,
        out_shape=jax.ShapeDtypeStruct((b, n), F32),
        grid=(n // tn,),
        in_specs=[pl.BlockSpec((b, d), lambda j: (0, 0)),
                  pl.BlockSpec((d, tn), lambda j: (0, j)),
                  pl.BlockSpec((1, tn), lambda j: (0, j))],
        out_specs=pl.BlockSpec((b, tn), lambda j: (0, j)),
        compiler_params=_cparams(1),
        name="ada",
    )(c, w_ada, b_ada.reshape(1, n))


def _in_proj_kernel(x_ref, mod_ref, g_ref, wqkv_ref, wg_ref, qkv_ref, gate_ref):
    h = _rms_mod(x_ref[0], g_ref[...], mod_ref[0, 0:1, :], mod_ref[0, 1:2, :]).astype(BF16)
    qkv_ref[0] = _dot(h, wqkv_ref[...]).astype(BF16)
    gate_ref[0] = _sigmoid(_dot(h, wg_ref[...])).astype(BF16)


def _in_proj(x, mod, g, wqkv, wg, tm=512):
    b, s, d = x.shape
    nq, ng = wqkv.shape[1], wg.shape[1]
    return pl.pallas_call(
        _in_proj_kernel,
        out_shape=(jax.ShapeDtypeStruct((b, s, nq), BF16), jax.ShapeDtypeStruct((b, s, ng), BF16)),
        grid=(b, s // tm),
        in_specs=[pl.BlockSpec((1, tm, d), lambda bi, i: (bi, i, 0)),
                  pl.BlockSpec((1, N_MOD, d), lambda bi, i: (bi, 0, 0)),
                  pl.BlockSpec((1, d), lambda bi, i: (0, 0)),
                  pl.BlockSpec((d, nq), lambda bi, i: (0, 0)),
                  pl.BlockSpec((d, ng), lambda bi, i: (0, 0))],
        out_specs=(pl.BlockSpec((1, tm, nq), lambda bi, i: (bi, i, 0)),
                   pl.BlockSpec((1, tm, ng), lambda bi, i: (bi, i, 0))),
        compiler_params=_cparams(2),
        name="in_proj",
    )(x, mod, g, wqkv, wg)


def _head_mask(h):
    lane = lax.broadcasted_iota(I32, (ATT_TILE, LANES), 1)
    return (lane >= h * HEAD_DIM) & (lane < (h + 1) * HEAD_DIM)


def _sb_kernel(q_ref, k_ref, v_ref, o_ref):
    t = ATT_TILE
    rows = HEAD_PAIR * t
    i = pl.program_id(2)
    row = lax.broadcasted_iota(I32, (t, t), 0)
    col = lax.broadcasted_iota(I32, (t, t), 1)
    later = jnp.where(row > col, 1.0, 0.0).astype(BF16)
    past = (lax.broadcasted_iota(I32, (rows, t), 1)
            < lax.broadcasted_iota(I32, (rows, t), 0) % t)
    q = q_ref[0]
    q2 = jnp.concatenate([jnp.where(_head_mask(h), q, jnp.zeros_like(q)) for h in range(HEAD_PAIR)], axis=0)

    def tile(j, carry, acc, diag):
        start = pl.multiple_of(j * t, t)
        kj = k_ref[0, pl.ds(start, t), :]
        vj = v_ref[0, pl.ds(start, t), :]
        z = _dot_nt(q2, kj)
        sp = jnp.maximum(z, 0.0) + jnp.log(1.0 + jnp.exp(-jnp.abs(z)))
        log_rest = -sp
        if diag:
            log_rest = jnp.where(past, log_rest, 0.0)
        hi = log_rest.astype(BF16)
        lo = (log_rest - hi.astype(F32)).astype(BF16)
        cum = _dot(jnp.concatenate([hi, lo], axis=0), later)
        a = jnp.exp(z - sp + cum[:rows] + cum[rows:] + carry)
        if diag:
            a = jnp.where(past, a, 0.0)
        return carry + jnp.sum(log_rest, axis=1, keepdims=True), acc + _dot(a.astype(BF16), vj)

    carry, acc = tile(i, jnp.zeros((rows, 1), F32), jnp.zeros((rows, LANES), F32), True)

    def live(st):
        j, carry, _ = st
        return (j >= 0) & (jnp.max(carry) > -SB_CUTOFF)

    def older(st):
        j, carry, acc = st
        carry, acc = tile(j, carry, acc, False)
        return j - 1, carry, acc

    _, _, acc = lax.while_loop(live, older, (i - 1, carry, acc))
    o_ref[0] = jnp.where(_head_mask(0), acc[:t], acc[t:]).astype(BF16)


def _sb_attn(qkv, q_col, k_col, v_col):
    b, s, _ = qkv.shape
    t = ATT_TILE
    npair = SB_HEADS // HEAD_PAIR
    return pl.pallas_call(
        _sb_kernel,
        out_shape=jax.ShapeDtypeStruct((b, s, SB_WIDTH), BF16),
        grid=(b, npair, s // t),
        in_specs=[pl.BlockSpec((1, t, LANES), lambda bi, p, i: (bi, i, q_col + p)),
                  pl.BlockSpec((1, s, LANES), lambda bi, p, i: (bi, 0, k_col + p)),
                  pl.BlockSpec((1, s, LANES), lambda bi, p, i: (bi, 0, v_col + p))],
        out_specs=pl.BlockSpec((1, t, LANES), lambda bi, p, i: (bi, i, p)),
        compiler_params=_cparams(3),
        name="sb_attn",
    )(qkv, qkv, qkv)


def _mb_kernel(qt_ref, k_ref, vt_ref, bias_ref, o_ref, kbar_ref, s0_ref, s1_ref, acc_ref, *, nb):
    t = ATT_TILE
    hd = HEAD_DIM
    i = pl.program_id(2)
    lane_row = lax.broadcasted_iota(I32, (1, LANES), 1)
    slot = lax.broadcasted_iota(I32, (LANES, t), 0)
    blk = lax.broadcasted_iota(I32, (hd, t), 0)

    @pl.when(i == 0)
    def _():
        kbar_ref[...] = jnp.zeros_like(kbar_ref)
        for n in range(nb):
            kbar_ref[n:n + 1, :] = jnp.mean(k_ref[0, n * t:(n + 1) * t, :].astype(F32), axis=0, keepdims=True)

    qt = qt_ref[0, 0]
    kb0, kb1, kb2 = _split3(kbar_ref[...])
    q_heads, not_sels = [], []
    for h in range(HEAD_PAIR):
        qh = jnp.where((slot >= h * hd) & (slot < (h + 1) * hd), qt, jnp.zeros_like(qt))
        gate = _dot(kb0, qh) + _dot(kb1, qh) + _dot(kb2, qh)
        gate = jnp.where(blk < i, gate, -jnp.inf)
        not_sel = jnp.ones((hd, t), F32)
        for _ in range(MB_TOPK):
            m = jnp.max(gate, axis=0, keepdims=True)
            idx = jnp.min(jnp.where(gate == m, blk, hd), axis=0, keepdims=True)
            pick = (blk == idx) & (m > -jnp.inf)
            not_sel = jnp.where(pick, 0.0, not_sel)
            gate = jnp.where(pick, -jnp.inf, gate)
        q_heads.append(qh)
        not_sels.append(not_sel.astype(BF16))
    q_ext = jnp.concatenate([jnp.concatenate(q_heads, axis=1), jnp.concatenate(not_sels, axis=1),
                             jnp.ones((ONES_ROWS, HEAD_PAIR * t), BF16),
                             jnp.zeros((hd - ONES_ROWS, HEAD_PAIR * t), BF16)], axis=0)
    ones_rows = jnp.ones((ONES_ROWS, t), BF16)

    def scores(j, masked, dead=None):
        kj = k_ref[0, pl.ds(pl.multiple_of(j * t, t), t), :]
        pen = jnp.where(lane_row == j, NEG, 0.0) if masked else jnp.zeros((1, LANES), F32)
        if dead is not None:
            pen = pen + jnp.where(lane_row == hd, jnp.where(dead, NEG, 0.0), 0.0)
        return _dot(jnp.concatenate([kj, jnp.broadcast_to(pen.astype(BF16), (t, LANES))], axis=1), q_ext)

    def weighted(j, p):
        return _dot(jnp.concatenate([vt_ref[0, j], ones_rows], axis=0), p)

    def probs(s, m_run):
        m_new = jnp.maximum(m_run, jnp.max(s, axis=0, keepdims=True))
        return m_new, jnp.exp(s - m_new).astype(BF16)

    i_prev = jnp.maximum(i - 1, 0)
    s_a = scores(i, False) + bias_ref[0, 0]
    s_b = scores(i_prev, True) + bias_ref[0, 1]
    m_a, p_a = probs(s_a, jnp.full((1, HEAD_PAIR * t), NEG, F32))
    m_b, p_b = probs(s_b, jnp.full((1, HEAD_PAIR * t), NEG, F32))
    acc_ref[0] = weighted(i, p_a)
    acc_ref[1] = weighted(i_prev, p_b)

    last_older = jnp.maximum(i - 2, 0)

    def pair_tiles(n):
        return jnp.minimum(2 * n, last_older), jnp.minimum(2 * n + 1, last_older)

    def stage_scores(n, slot_ref):
        j_a, j_b = pair_tiles(n)
        slot_ref[0] = scores(j_a, True, 2 * n >= i - 1)
        slot_ref[1] = scores(j_b, True, 2 * n + 1 >= i - 1)

    def absorb(n, slot_ref, m_a, m_b):
        j_a, j_b = pair_tiles(n)
        m_a2, p_a = probs(slot_ref[0], m_a)
        m_b2, p_b = probs(slot_ref[1], m_b)
        acc_ref[0] = jnp.exp(m_a - m_a2) * acc_ref[0] + weighted(j_a, p_a)
        acc_ref[1] = jnp.exp(m_b - m_b2) * acc_ref[1] + weighted(j_b, p_b)
        return m_a2, m_b2

    stage_scores(0, s0_ref)

    def older(n2, c):
        m_a, m_b = c
        stage_scores(2 * n2 + 1, s1_ref)
        m_a, m_b = absorb(2 * n2, s0_ref, m_a, m_b)
        stage_scores(2 * n2 + 2, s0_ref)
        return absorb(2 * n2 + 1, s1_ref, m_a, m_b)

    m_a, m_b = lax.fori_loop(0, (i // 2 + 1) // 2, older, (m_a, m_b))
    m_run = jnp.maximum(m_a, m_b)
    acc = jnp.exp(m_a - m_run) * acc_ref[0] + jnp.exp(m_b - m_run) * acc_ref[1]
    denom = acc[LANES:LANES + 1, :]
    out_t = jnp.concatenate([acc[h * hd:(h + 1) * hd, h * t:(h + 1) * t] / denom[:, h * t:(h + 1) * t]
                             for h in range(HEAD_PAIR)], axis=0)
    o_ref[0] = out_t.T.astype(BF16)


def _bias_tab_kernel(rb_ref, bucket_ref, o_ref):
    h = pl.program_id(0)
    t = ATT_TILE
    far = rb_ref[h, REL_BUCKETS - 1]
    causal = lax.broadcasted_iota(I32, (t, t), 1) >= lax.broadcasted_iota(I32, (t, t), 0)
    for w in range(2):
        bk = bucket_ref[w]
        tile = jnp.zeros((t, t), F32)
        for b in range(REL_BUCKETS):
            tile = jnp.where(bk == b, rb_ref[h, b] - far, tile)
        o_ref[0, w] = jnp.where(causal, tile, NEG) if w == 0 else tile


def _moba_bias_table(rel_bias):
    assert MB_BLOCK >= REL_MAX_DIST and ATT_TILE == MB_BLOCK
    t = ATT_TILE
    nh = rel_bias.shape[0]
    d = jnp.arange(t, dtype=I32)[None, :] - jnp.arange(t, dtype=I32)[:, None]
    bucket = jnp.stack([_t5_bucket(d), _t5_bucket(d + t)]).astype(I32)
    return pl.pallas_call(
        _bias_tab_kernel,
        out_shape=jax.ShapeDtypeStruct((nh // HEAD_PAIR, 2, t, HEAD_PAIR * t), F32),
        grid=(nh,),
        in_specs=[pl.BlockSpec(memory_space=pltpu.SMEM),
                  pl.BlockSpec((2, t, t), lambda h: (0, 0, 0))],
        out_specs=pl.BlockSpec((1, 2, t, t), lambda h: (h // HEAD_PAIR, 0, 0, h % HEAD_PAIR)),
        compiler_params=_cparams(1),
        name="bias_tab",
    )(rel_bias, bucket)


def _blocked_t(a):
    b, s, w = a.shape
    return a.reshape(b, s // ATT_TILE, ATT_TILE, w).transpose(0, 1, 3, 2)


def _mb_attn(qkv, bias_tab, q_col, k_col, v_col):
    b, s, _ = qkv.shape
    t = ATT_TILE
    nb = s // t
    assert t == MB_BLOCK and s % t == 0 and nb <= HEAD_DIM and HEAD_PAIR == 2
    npair = MB_HEADS // HEAD_PAIR
    qt = _blocked_t(qkv[:, :, q_col * LANES:q_col * LANES + MB_WIDTH])
    vt = _blocked_t(qkv[:, :, v_col * LANES:v_col * LANES + MB_WIDTH])
    return pl.pallas_call(
        functools.partial(_mb_kernel, nb=nb),
        out_shape=jax.ShapeDtypeStruct((b, s, MB_WIDTH), BF16),
        grid=(b, npair, nb),
        in_specs=[pl.BlockSpec((1, 1, LANES, t), lambda bi, p, i: (bi, i, p, 0)),
                  pl.BlockSpec((1, s, LANES), lambda bi, p, i: (bi, 0, k_col + p)),
                  pl.BlockSpec((1, nb, LANES, t), lambda bi, p, i: (bi, 0, p, 0)),
                  pl.BlockSpec((1, 2, t, HEAD_PAIR * t), lambda bi, p, i: (p, 0, 0, 0))],
        out_specs=pl.BlockSpec((1, t, LANES), lambda bi, p, i: (bi, i, p)),
        scratch_shapes=[pltpu.VMEM((HEAD_DIM, LANES), F32),
                        pltpu.VMEM((2, t, HEAD_PAIR * t), F32),
                        pltpu.VMEM((2, t, HEAD_PAIR * t), F32),
                        pltpu.VMEM((2, LANES + ONES_ROWS, HEAD_PAIR * t), F32)],
        compiler_params=_cparams(3),
        name="mb_attn",
    )(qt, qkv, vt, bias_tab)


def _t5_bucket(dist):
    n = jnp.maximum(dist, 0)
    max_exact = REL_BUCKETS // 2
    nf = jnp.maximum(n, 1).astype(F32)
    large = max_exact + (jnp.log(nf / max_exact) / jnp.log(jnp.float32(REL_MAX_DIST / max_exact))
                         * (REL_BUCKETS - max_exact)).astype(I32)
    large = jnp.minimum(large, REL_BUCKETS - 1)
    return jnp.where(n < max_exact, n, large)


def _out_proj_kernel(osb_ref, omb_ref, gate_ref, x_ref, mod_ref, g2_ref, wsb_ref, wmb_ref, wout_ref,
                     x1_ref, h2_ref, h2p_ref):
    d = x_ref.shape[-1]
    gates = gate_ref[0].astype(F32)
    merged = (gates[:, :d] * _dot(osb_ref[0], wsb_ref[...])
              + gates[:, d:] * _dot(omb_ref[0], wmb_ref[...]))
    x1 = x_ref[0] + mod_ref[0, 2:3, :] * _dot(merged.astype(BF16), wout_ref[...])
    x1_ref[0] = x1
    h2 = _rms_mod(x1, g2_ref[...], mod_ref[0, 3:4, :], mod_ref[0, 4:5, :])
    h2_ref[0] = h2
    h2p_ref[0] = _pack_halves(h2)


def _out_proj(osb, omb, gates, x, mod, g2, wsb, wmb, wout, tm=512):
    b, s, d = x.shape
    row = lambda w: pl.BlockSpec((1, tm, w), lambda bi, i: (bi, i, 0))
    full = lambda a: pl.BlockSpec(a.shape, lambda bi, i: (0,) * a.ndim)
    return pl.pallas_call(
        _out_proj_kernel,
        out_shape=(jax.ShapeDtypeStruct((b, s, d), F32), jax.ShapeDtypeStruct((b, s, d), F32),
                   jax.ShapeDtypeStruct((b, s, d // 2), I32)),
        grid=(b, s // tm),
        in_specs=[row(osb.shape[-1]), row(omb.shape[-1]), row(gates.shape[-1]), row(d),
                  pl.BlockSpec((1, N_MOD, d), lambda bi, i: (bi, 0, 0)),
                  full(g2), full(wsb), full(wmb), full(wout)],
        out_specs=(row(d), row(d), row(d // 2)),
        compiler_params=_cparams(2),
        name="out_proj",
    )(osb, omb, gates, x, mod, g2, wsb, wmb, wout)


def _first_argmax(v, iota, n):
    m = jnp.max(v, axis=0, keepdims=True)
    idx = jnp.min(jnp.where(v == m, iota, n), axis=0, keepdims=True)
    return m, idx


def _route_kernel(h_ref, wr_ref, rb_ref, eidx_ref, wts_ref, slot_ref, cnt_ref, carry_ref):
    tm = h_ref.shape[0]
    e, g, gs = N_EXPERTS, N_GROUPS, GROUP_SIZE
    step = pl.program_id(0)

    @pl.when(step == 0)
    def _():
        carry_ref[...] = jnp.zeros_like(carry_ref)

    h0, h1, _ = _split3(h_ref[...])
    w0, w1, _ = _split3(wr_ref[...])
    logits = _dot_nt(w0, h0) + _dot_nt(w0, h1) + _dot_nt(w1, h0)
    scores = _sigmoid(logits)
    choice = scores + rb_ref[...]

    c3 = choice.reshape(g, gs, tm)
    sub = lax.broadcasted_iota(I32, (g, gs, tm), 1)
    m1 = jnp.max(c3, axis=1, keepdims=True)
    i1 = jnp.min(jnp.where(c3 == m1, sub, gs), axis=1, keepdims=True)
    m2 = jnp.max(jnp.where(sub == i1, -jnp.inf, c3), axis=1, keepdims=True)
    gscore = (m1 + m2).reshape(g, tm)

    giota = lax.broadcasted_iota(I32, (g, tm), 0)
    gmask = jnp.zeros((g, tm), F32)
    for _ in range(TOPK_GROUPS):
        _, gi = _first_argmax(gscore, giota, g)
        pick = giota == gi
        gmask = jnp.where(pick, 1.0, gmask)
        gscore = jnp.where(pick, -jnp.inf, gscore)
    emask = jnp.broadcast_to(gmask.reshape(g, 1, tm), (g, gs, tm)).reshape(e, tm)
    masked = jnp.where(emask > 0.5, choice, NEG)

    eiota = lax.broadcasted_iota(I32, (e, tm), 0)
    chosen = jnp.zeros((e, tm), F32)
    idxs, ws = [], []
    for _ in range(TOPK_EXPERTS):
        _, ei = _first_argmax(masked, eiota, e)
        pick = eiota == ei
        idxs.append(ei)
        ws.append(jnp.sum(jnp.where(pick, scores, 0.0), axis=0, keepdims=True))
        chosen = jnp.where(pick, 1.0, chosen)
        masked = jnp.where(pick, -jnp.inf, masked)
    wsum = ws[0]
    for w in ws[1:]:
        wsum = wsum + w

    r = lax.broadcasted_iota(I32, (tm, tm), 0)
    c = lax.broadcasted_iota(I32, (tm, tm), 1)
    before = jnp.where(r < c, 1.0, 0.0).astype(BF16)
    prefix = _dot(chosen.astype(BF16), before) + carry_ref[...]
    for k in range(TOPK_EXPERTS):
        eidx_ref[k:k + 1, :] = idxs[k]
        wts_ref[k:k + 1, :] = ws[k] / wsum * ROUTED_SCALE
        slot = jnp.sum(jnp.where(eiota == idxs[k], prefix, 0.0), axis=0, keepdims=True)
        slot_ref[k:k + 1, :] = slot.astype(I32)
    carry_ref[...] = carry_ref[...] + jnp.sum(chosen, axis=1, keepdims=True)
    cnt_ref[...] = carry_ref[...].astype(I32)


def _route(h2, w_router_t, router_bias, tm=512):
    t, d = h2.shape
    e = N_EXPERTS
    kk = TOPK_EXPERTS
    tok = lambda: pl.BlockSpec((kk, tm), lambda i: (0, i))
    return pl.pallas_call(
        _route_kernel,
        out_shape=(jax.ShapeDtypeStruct((kk, t), I32), jax.ShapeDtypeStruct((kk, t), F32),
                   jax.ShapeDtypeStruct((kk, t), I32), jax.ShapeDtypeStruct((e, 1), I32)),
        grid=(t // tm,),
        in_specs=[pl.BlockSpec((tm, d), lambda i: (i, 0)),
                  pl.BlockSpec((e, d), lambda i: (0, 0)),
                  pl.BlockSpec((e, 1), lambda i: (0, 0))],
        out_specs=(tok(), tok(), tok(), pl.BlockSpec((e, 1), lambda i: (0, 0))),
        scratch_shapes=[pltpu.VMEM((e, 1), F32)],
        compiler_params=_cparams(1),
        name="route",
    )(h2, w_router_t, router_bias.reshape(e, 1))


def _dest_kernel(eidx_ref, slot_ref, pstart_ref, o_ref):
    kk, tm = eidx_ref.shape
    eiota = lax.broadcasted_iota(I32, (N_EXPERTS, tm), 0)
    for k in range(kk):
        base = jnp.sum(jnp.where(eiota == eidx_ref[k:k + 1, :], pstart_ref[...], 0), axis=0, keepdims=True)
        o_ref[k:k + 1, :] = base + slot_ref[k:k + 1, :]


def _dest(eidx, slot, pstart, tm=2048):
    kk, t = eidx.shape
    tm = min(tm, t)
    tok = pl.BlockSpec((kk, tm), lambda i: (0, i))
    return pl.pallas_call(
        _dest_kernel,
        out_shape=jax.ShapeDtypeStruct((kk, t), I32),
        grid=(t // tm,),
        in_specs=[tok, tok, pl.BlockSpec((N_EXPERTS, 1), lambda i: (0, 0))],
        out_specs=tok,
        compiler_params=_cparams(1),
        name="dest",
    )(eidx, slot, pstart.reshape(N_EXPERTS, 1))


def _pack_halves(x):
    n = x.shape[1] // 2
    hi = lax.bitcast_convert_type(x[:, :n].astype(BF16).astype(F32), I32)
    lo = lax.bitcast_convert_type(x[:, n:].astype(BF16).astype(F32), I32)
    return hi | lax.shift_right_logical(lo, jnp.full_like(lo, 16))


def _unpack_halves(p):
    hi = lax.bitcast_convert_type(p & jnp.int32(-65536), F32)
    lo = lax.bitcast_convert_type(p << 16, F32)
    return hi, lo


def _sc_gather_rows(table, idx):
    n = idx.shape[0]
    w = table.shape[1]
    workers = SC_CORES * SC_SUBCORES
    assert n % (workers * SC_WINDOW) == 0
    per_worker = n // workers
    mesh = plsc.VectorSubcoreMesh(core_axis_name="c", subcore_axis_name="s")

    @functools.partial(
        pl.kernel, mesh=mesh,
        out_type=jax.ShapeDtypeStruct((n, w), table.dtype),
        scratch_types=[pltpu.VMEM((SC_WINDOW,), I32), pltpu.VMEM((SC_WINDOW, w), table.dtype),
                       pltpu.SemaphoreType.DMA],
    )
    def gather(table_hbm, idx_hbm, out_hbm, idx_v, rows_v, sem):
        wid = lax.axis_index("s") * SC_CORES + lax.axis_index("c")

        @pl.loop(0, per_worker // SC_WINDOW)
        def _(g):
            base = wid * per_worker + g * SC_WINDOW
            pltpu.sync_copy(idx_hbm.at[pl.ds(base, SC_WINDOW)], idx_v)
            pltpu.async_copy(table_hbm.at[idx_v], rows_v, sem).wait()
            pltpu.sync_copy(rows_v, out_hbm.at[pl.ds(base, SC_WINDOW)])

    return gather(table, idx)


def _sc_scatter_rows(rows, idx, n_out):
    t, w = rows.shape
    kk = idx.shape[0]
    win = LANES
    workers = SC_CORES * SC_SUBCORES
    assert t % (workers * win) == 0
    per_worker = t // workers
    mesh = plsc.VectorSubcoreMesh(core_axis_name="c", subcore_axis_name="s")

    @functools.partial(
        pl.kernel, mesh=mesh,
        out_type=jax.ShapeDtypeStruct((n_out, w), rows.dtype),
        scratch_types=[pltpu.VMEM((kk, win), I32), pltpu.VMEM((win, w), rows.dtype),
                       pltpu.SemaphoreType.DMA],
    )
    def scatter(rows_hbm, idx_hbm, out_hbm, idx_v, rows_v, sem):
        wid = lax.axis_index("s") * SC_CORES + lax.axis_index("c")

        @pl.loop(0, per_worker // win)
        def _(g):
            base = wid * per_worker + g * win
            pltpu.sync_copy(idx_hbm.at[:, pl.ds(base, win)], idx_v)
            pltpu.sync_copy(rows_hbm.at[pl.ds(base, win)], rows_v)
            for k in range(kk):
                pltpu.async_copy(rows_v, out_hbm.at[idx_v.at[k]], sem).wait()

    return scatter(rows, idx)


def _ffn_kernel(be_ref, valid_ref, xs_ref, wg_ref, wu_ref, wd_ref, ys_ref, wgu_ref, wdn_ref):
    i = pl.program_id(0)
    ff = wg_ref.shape[-1]
    rb, half = xs_ref.shape

    @pl.when((i == 0) | (be_ref[i] != be_ref[jnp.maximum(i - 1, 0)]))
    def _():
        wgu_ref[:, :ff] = wg_ref[0].astype(BF16)
        wgu_ref[:, ff:] = wu_ref[0].astype(BF16)
        wdn_ref[...] = wd_ref[0].astype(BF16)

    @pl.when(valid_ref[i] > 0)
    def _():
        is_token = lax.broadcasted_iota(I32, (rb, half), 0) < valid_ref[i]
        x_hi, x_lo = _unpack_halves(jnp.where(is_token, xs_ref[...], jnp.zeros((rb, half), xs_ref.dtype)))
        au = _dot(x_hi.astype(BF16), wgu_ref[:half, :]) + _dot(x_lo.astype(BF16), wgu_ref[half:, :])
        a, u = au[:, :ff], au[:, ff:]
        act = (a * _sigmoid(a) * u).astype(BF16)
        ys_ref[...] = _pack_halves(_dot(act, wdn_ref[...]))

    @pl.when(valid_ref[i] <= 0)
    def _():
        ys_ref[...] = jnp.zeros_like(ys_ref)


def _ffn(blk_expert, blk_valid, xs, wg, wu, wd):
    n_rows, w = xs.shape
    rb = ROW_BLOCK
    _, d, ff = wg.shape
    return pl.pallas_call(
        _ffn_kernel,
        out_shape=jax.ShapeDtypeStruct((n_rows, w), I32),
        grid_spec=pltpu.PrefetchScalarGridSpec(
            num_scalar_prefetch=2,
            grid=(n_rows // rb,),
            in_specs=[pl.BlockSpec((rb, w), lambda i, be, bv: (i, 0)),
                      pl.BlockSpec((1, d, ff), lambda i, be, bv: (be[i], 0, 0)),
                      pl.BlockSpec((1, d, ff), lambda i, be, bv: (be[i], 0, 0)),
                      pl.BlockSpec((1, ff, d), lambda i, be, bv: (be[i], 0, 0))],
            out_specs=pl.BlockSpec((rb, w), lambda i, be, bv: (i, 0)),
            scratch_shapes=[pltpu.VMEM((d, 2 * ff), BF16), pltpu.VMEM((ff, d), BF16)],
        ),
        compiler_params=_cparams(1),
        name="ffn",
    )(blk_expert, blk_valid, xs, wg, wu, wd)


def _combine_kernel(rows_ref, wts_ref, h_ref, x1_ref, gate2_ref, fg_ref, wgus_ref, wds_ref, o_ref):
    half = h_ref.shape[-1]
    ff = wds_ref.shape[0]
    h_hi, h_lo = _unpack_halves(h_ref[0])
    au = _dot(h_hi.astype(BF16), wgus_ref[:half, :]) + _dot(h_lo.astype(BF16), wgus_ref[half:, :])
    a, u = au[:, :ff], au[:, ff:]
    y = _dot((a * _sigmoid(a) * u).astype(BF16), wds_ref[...])

    w = wts_ref[0]
    y_hi, y_lo = y[:, :half], y[:, half:]
    for k in range(TOPK_EXPERTS):
        r_hi, r_lo = _unpack_halves(rows_ref[k, 0])
        y_hi = y_hi + w[:, k:k + 1] * r_hi
        y_lo = y_lo + w[:, k:k + 1] * r_lo
    y = jnp.concatenate([y_hi, y_lo], axis=1)
    x2 = x1_ref[0] + gate2_ref[0] * y
    o_ref[0] = x2 * lax.rsqrt(jnp.mean(x2 * x2, axis=-1, keepdims=True) + EPS) * fg_ref[...]


def _combine(rows, wts, h2p, x1, gate2, final_g, wgus, wds, tb=256):
    b, s, d = x1.shape
    kk = rows.shape[0]
    w = h2p.shape[-1]
    row = lambda n: pl.BlockSpec((1, tb, n), lambda bi, i: (bi, i, 0))
    full = lambda a: pl.BlockSpec(a.shape, lambda bi, i: (0,) * a.ndim)
    return pl.pallas_call(
        _combine_kernel,
        out_shape=jax.ShapeDtypeStruct((b, s, d), F32),
        grid=(b, s // tb),
        in_specs=[pl.BlockSpec((kk, 1, tb, w), lambda bi, i: (0, bi, i, 0)),
                  row(kk), row(w), row(d),
                  pl.BlockSpec((1, 1, d), lambda bi, i: (bi, 0, 0)),
                  full(final_g), full(wgus), full(wds)],
        out_specs=row(d),
        compiler_params=_cparams(2),
        name="combine",
    )(rows, wts, h2p, x1, gate2, final_g, wgus, wds)


def _layer(x, mod, norm1_g, norm2_g, w_in, w_branch_sb, w_branch_mb, w_out, w_router, router_bias,
           w_gate_e, w_up_e, w_down_e, w_gate_sh, w_up_sh, w_down_sh, bias_tab, final_g):
    b, s, d = x.shape
    t = b * s
    nqkv = 3 * SB_WIDTH + 3 * MB_WIDTH

    col = jnp.arange(nqkv)
    is_q = (col < SB_WIDTH) | ((col >= 3 * SB_WIDTH) & (col < 3 * SB_WIDTH + MB_WIDTH))
    wqkv = (w_in[:, :nqkv] * jnp.where(is_q, HEAD_DIM ** -0.5, 1.0)).astype(BF16)
    wg = w_in[:, nqkv:].astype(BF16)
    qkv, gates = _in_proj(x, mod, norm1_g.reshape(1, d), wqkv, wg)

    cb = SB_WIDTH // LANES
    o_sb = _sb_attn(qkv, 0, cb, 2 * cb)
    o_mb = _mb_attn(qkv, bias_tab, 3 * cb, 4 * cb, 5 * cb)

    x1, h2, h2p = _out_proj(o_sb, o_mb, gates, x, mod, norm2_g.reshape(1, d), w_branch_sb.astype(BF16),
                            w_branch_mb.astype(BF16), w_out.astype(BF16))

    eidx, wts, slot, counts = _route(h2.reshape(t, d), w_router.T, router_bias)

    rb = ROW_BLOCK
    counts = counts.reshape(N_EXPERTS)
    pcounts = (counts + rb - 1) // rb * rb
    pend = jnp.cumsum(pcounts)
    pstart = pend - pcounts
    n_blk = t * TOPK_EXPERTS // rb + N_EXPERTS
    blk_row = jnp.arange(n_blk, dtype=I32) * rb
    blk_expert = jnp.minimum(jnp.sum(pend[None, :] <= blk_row[:, None], axis=1), N_EXPERTS - 1).astype(I32)
    blk_valid = jnp.clip((pstart + counts)[blk_expert] - blk_row, 0, rb).astype(I32)
    dest = _dest(eidx, slot, pstart.astype(I32))

    xs = _sc_scatter_rows(h2p.reshape(t, d // 2), dest, n_blk * rb)
    ys = _ffn(blk_expert, blk_valid, xs, w_gate_e, w_up_e, w_down_e)
    rows = _sc_gather_rows(ys, dest.reshape(TOPK_EXPERTS * t)).reshape(TOPK_EXPERTS, b, s, d // 2)
    wts_tok = wts.T.reshape(b, s, TOPK_EXPERTS)
    wgus = jnp.concatenate([w_gate_sh, w_up_sh], axis=1).astype(BF16)
    return _combine(rows, wts_tok, h2p, x1, mod[:, 5:6, :], final_g.reshape(1, d), wgus, w_down_sh.astype(BF16))


def kernel(x, c, norm1_g, norm2_g, w_ada, b_ada, w_in, w_branch_sb, w_branch_mb, w_out, w_router, router_bias,
           w_gate_e, w_up_e, w_down_e, w_gate_sh, w_up_sh, w_down_sh, rel_bias, final_g):
    b, s, d = x.shape
    depth = w_ada.shape[0]
    assert depth == 1, "the final norm is fused into the only layer's combine step"
    bias_tab = _moba_bias_table(rel_bias)
    l = 0
    mod = _ada(c, w_ada[l], b_ada[l]).reshape(b, N_MOD, d)
    return _layer(x, mod, norm1_g[l], norm2_g[l], w_in[l], w_branch_sb[l], w_branch_mb[l], w_out[l],
                  w_router[l], router_bias[l], w_gate_e[l], w_up_e[l], w_down_e[l], w_gate_sh[l],
                  w_up_sh[l], w_down_sh[l], bias_tab, final_g)
```

```python
import functools

import jax
import jax.numpy as jnp
from jax import lax
from jax.experimental import pallas as pl
from jax.experimental.pallas import tpu as pltpu
from jax.experimental.pallas import tpu_sc as plsc

F32 = jnp.float32
BF16 = jnp.bfloat16
I32 = jnp.int32

HEAD_DIM = 64
SB_HEADS = 8
MB_HEADS = 8
SB_WIDTH = SB_HEADS * HEAD_DIM
MB_WIDTH = MB_HEADS * HEAD_DIM
MB_BLOCK = 256
MB_TOPK = 3
REL_BUCKETS = 32
REL_MAX_DIST = 128
N_EXPERTS = 64
N_GROUPS = 8
GROUP_SIZE = N_EXPERTS // N_GROUPS
TOPK_GROUPS = 4
TOPK_EXPERTS = 8
ROUTED_SCALE = 2.5
N_MOD = 6
EPS = 1e-6
NEG = -1e30

LANES = 128
HEAD_PAIR = LANES // HEAD_DIM
ATT_TILE = 256
SB_CUTOFF = 120.0
ONES_ROWS = 16
ROW_BLOCK = 512

SC_CORES = 2
SC_SUBCORES = 16
SC_WINDOW = 64
VMEM_LIMIT = 56 * 1024 * 1024


def _cparams(n_axes):
    return pltpu.CompilerParams(dimension_semantics=("arbitrary",) * n_axes,
                                vmem_limit_bytes=VMEM_LIMIT)


def _dot(a, b):
    return jnp.dot(a, b, preferred_element_type=F32)


def _dot_nt(a, b):
    return lax.dot_general(a, b, (((1,), (1,)), ((), ())), preferred_element_type=F32)


def _split3(a):
    p0 = a.astype(BF16)
    r0 = a - p0.astype(F32)
    p1 = r0.astype(BF16)
    p2 = (r0 - p1.astype(F32)).astype(BF16)
    return p0, p1, p2


def _sigmoid(x):
    return 1.0 / (1.0 + jnp.exp(-x))


def _rms_mod(x, g, shift, scale):
    y = x * lax.rsqrt(jnp.mean(x * x, axis=-1, keepdims=True) + EPS) * g
    return y * (1.0 + scale) + shift


def _ada_kernel(c_ref, w_ref, b_ref, o_ref):
    c = c_ref[...]
    s = c * _sigmoid(c)
    s0, s1, _ = _split3(s)
    w0, w1, _ = _split3(w_ref[...])
    o_ref[...] = _dot(s0, w0) + _dot(s0, w1) + _dot(s1, w0) + b_ref[...]


def _ada(c, w_ada, b_ada):
    b, d = c.shape
    n = w_ada.shape[1]
    tn = 1536
    return pl.pallas_call(
        _ada_kernel,
        out_shape=jax.ShapeDtypeStruct((b, n), F32),
        grid=(n // tn,),
        in_specs=[pl.BlockSpec((b, d), lambda j: (0, 0)),
                  pl.BlockSpec((d, tn), lambda j: (0, j)),
                  pl.BlockSpec((1, tn), lambda j: (0, j))],
        out_specs=pl.BlockSpec((b, tn), lambda j: (0, j)),
        compiler_params=_cparams(1),
        name="ada",
    )(c, w_ada, b_ada.reshape(1, n))


def _in_proj_kernel(x_ref, mod_ref, g_ref, wqkv_ref, wg_ref, qkv_ref, gate_ref):
    h = _rms_mod(x_ref[0], g_ref[...], mod_ref[0, 0:1, :], mod_ref[0, 1:2, :]).astype(BF16)
    qkv_ref[0] = _dot(h, wqkv_ref[...]).astype(BF16)
    gate_ref[0] = _sigmoid(_dot(h, wg_ref[...])).astype(BF16)


def _in_proj(x, mod, g, wqkv, wg, tm=512):
    b, s, d = x.shape
    nq, ng = wqkv.shape[1], wg.shape[1]
    return pl.pallas_call(
        _in_proj_kernel,
        out_shape=(jax.ShapeDtypeStruct((b, s, nq), BF16), jax.ShapeDtypeStruct((b, s, ng), BF16)),
        grid=(b, s // tm),
        in_specs=[pl.BlockSpec((1, tm, d), lambda bi, i: (bi, i, 0)),
                  pl.BlockSpec((1, N_MOD, d), lambda bi, i: (bi, 0, 0)),
                  pl.BlockSpec((1, d), lambda bi, i: (0, 0)),
                  pl.BlockSpec((d, nq), lambda bi, i: (0, 0)),
                  pl.BlockSpec((d, ng), lambda bi, i: (0, 0))],
        out_specs=(pl.BlockSpec((1, tm, nq), lambda bi, i: (bi, i, 0)),
                   pl.BlockSpec((1, tm, ng), lambda bi, i: (bi, i, 0))),
        compiler_params=_cparams(2),
        name="in_proj",
    )(x, mod, g, wqkv, wg)


def _head_mask(h):
    lane = lax.broadcasted_iota(I32, (ATT_TILE, LANES), 1)
    return (lane >= h * HEAD_DIM) & (lane < (h + 1) * HEAD_DIM)


def _sb_kernel(q_ref, k_ref, v_ref, o_ref):
    t = ATT_TILE
    rows = HEAD_PAIR * t
    i = pl.program_id(2)
    row = lax.broadcasted_iota(I32, (t, t), 0)
    col = lax.broadcasted_iota(I32, (t, t), 1)
    later = jnp.where(row > col, 1.0, 0.0).astype(BF16)
    past = (lax.broadcasted_iota(I32, (rows, t), 1)
            < lax.broadcasted_iota(I32, (rows, t), 0) % t)
    q = q_ref[0]
    q2 = jnp.concatenate([jnp.where(_head_mask(h), q, jnp.zeros_like(q)) for h in range(HEAD_PAIR)], axis=0)

    def local(j, diag):
        start = pl.multiple_of(j * t, t)
        kj = k_ref[0, pl.ds(start, t), :]
        z = _dot_nt(q2, kj)
        sp = jnp.maximum(z, 0.0) + jnp.log(1.0 + jnp.exp(-jnp.abs(z)))
        log_rest = -sp
        if diag:
            log_rest = jnp.where(past, log_rest, 0.0)
        hi = log_rest.astype(BF16)
        lo = (log_rest - hi.astype(F32)).astype(BF16)
        cum = _dot(jnp.concatenate([hi, lo], axis=0), later)
        return z - sp, log_rest, cum[:rows] + cum[rows:], v_ref[0, pl.ds(start, t), :]

    has_prev = i > 0
    lb_d, rest_d, skip_d, v_d = local(i, True)
    lb_p, rest_p, skip_p, v_p = local(jnp.maximum(i - 1, 0), False)
    a_d = jnp.where(past, jnp.exp(lb_d + skip_d), 0.0)
    carry = jnp.sum(rest_d, axis=1, keepdims=True)
    a_p = jnp.where(has_prev, jnp.exp(lb_p + skip_p + carry), 0.0)
    acc = _dot(a_d.astype(BF16), v_d) + _dot(a_p.astype(BF16), v_p)
    carry = carry + jnp.where(has_prev, jnp.sum(rest_p, axis=1, keepdims=True), 0.0)

    def live(st):
        j, carry, _ = st
        return (j >= 0) & (jnp.max(carry) > -SB_CUTOFF)

    def older(st):
        j, carry, acc = st
        lb, rest, skip, vj = local(j, False)
        a = jnp.exp(lb + skip + carry)
        return j - 1, carry + jnp.sum(rest, axis=1, keepdims=True), acc + _dot(a.astype(BF16), vj)

    _, _, acc = lax.while_loop(live, older, (i - 2, carry, acc))
    o_ref[0] = jnp.where(_head_mask(0), acc[:t], acc[t:]).astype(BF16)


def _sb_attn(qkv, q_col, k_col, v_col):
    b, s, _ = qkv.shape
    t = ATT_TILE
    npair = SB_HEADS // HEAD_PAIR
    return pl.pallas_call(
        _sb_kernel,
        out_shape=jax.ShapeDtypeStruct((b, s, SB_WIDTH), BF16),
        grid=(b, npair, s // t),
        in_specs=[pl.BlockSpec((1, t, LANES), lambda bi, p, i: (bi, i, q_col + p)),
                  pl.BlockSpec((1, s, LANES), lambda bi, p, i: (bi, 0, k_col + p)),
                  pl.BlockSpec((1, s, LANES), lambda bi, p, i: (bi, 0, v_col + p))],
        out_specs=pl.BlockSpec((1, t, LANES), lambda bi, p, i: (bi, i, p)),
        compiler_params=_cparams(3),
        name="sb_attn",
    )(qkv, qkv, qkv)


def _mb_kernel(qt_ref, k_ref, vt_ref, bias_ref, o_ref, kbar_ref, s0_ref, s1_ref, acc_ref, *, nb):
    t = ATT_TILE
    hd = HEAD_DIM
    i = pl.program_id(2)
    lane_row = lax.broadcasted_iota(I32, (1, LANES), 1)
    slot = lax.broadcasted_iota(I32, (LANES, t), 0)
    blk = lax.broadcasted_iota(I32, (hd, t), 0)

    @pl.when(i == 0)
    def _():
        kbar_ref[...] = jnp.zeros_like(kbar_ref)
        for n in range(nb):
            kbar_ref[n:n + 1, :] = jnp.mean(k_ref[0, n * t:(n + 1) * t, :].astype(F32), axis=0, keepdims=True)

    qt = qt_ref[0, 0]
    kb0, kb1, kb2 = _split3(kbar_ref[...])
    q_heads, not_sels = [], []
    for h in range(HEAD_PAIR):
        qh = jnp.where((slot >= h * hd) & (slot < (h + 1) * hd), qt, jnp.zeros_like(qt))
        gate = _dot(kb0, qh) + _dot(kb1, qh) + _dot(kb2, qh)
        gate = jnp.where(blk < i, gate, -jnp.inf)
        not_sel = jnp.ones((hd, t), F32)
        for _ in range(MB_TOPK):
            m = jnp.max(gate, axis=0, keepdims=True)
            idx = jnp.min(jnp.where(gate == m, blk, hd), axis=0, keepdims=True)
            pick = (blk == idx) & (m > -jnp.inf)
            not_sel = jnp.where(pick, 0.0, not_sel)
            gate = jnp.where(pick, -jnp.inf, gate)
        q_heads.append(qh)
        not_sels.append(not_sel.astype(BF16))
    q_ext = jnp.concatenate([jnp.concatenate(q_heads, axis=1), jnp.concatenate(not_sels, axis=1),
                             jnp.ones((ONES_ROWS, HEAD_PAIR * t), BF16),
                             jnp.zeros((hd - ONES_ROWS, HEAD_PAIR * t), BF16)], axis=0)
    ones_rows = jnp.ones((ONES_ROWS, t), BF16)

    def scores(j, masked, dead=None):
        kj = k_ref[0, pl.ds(pl.multiple_of(j * t, t), t), :]
        pen = jnp.where(lane_row == j, NEG, 0.0) if masked else jnp.zeros((1, LANES), F32)
        if dead is not None:
            pen = pen + jnp.where(lane_row == hd, jnp.where(dead, NEG, 0.0), 0.0)
        return _dot(jnp.concatenate([kj, jnp.broadcast_to(pen.astype(BF16), (t, LANES))], axis=1), q_ext)

    def weighted(j, p):
        return _dot(jnp.concatenate([vt_ref[0, j], ones_rows], axis=0), p)

    def probs(s, m_run):
        m_new = jnp.maximum(m_run, jnp.max(s, axis=0, keepdims=True))
        return m_new, jnp.exp(s - m_new).astype(BF16)

    i_prev = jnp.maximum(i - 1, 0)
    s_a = scores(i, False) + bias_ref[0, 0]
    s_b = scores(i_prev, True) + bias_ref[0, 1]
    m_a, p_a = probs(s_a, jnp.full((1, HEAD_PAIR * t), NEG, F32))
    m_b, p_b = probs(s_b, jnp.full((1, HEAD_PAIR * t), NEG, F32))
    acc_ref[0] = weighted(i, p_a)
    acc_ref[1] = weighted(i_prev, p_b)

    last_older = jnp.maximum(i - 2, 0)

    def pair_tiles(n):
        return jnp.minimum(2 * n, last_older), jnp.minimum(2 * n + 1, last_older)

    def stage_scores(n, slot_ref):
        j_a, j_b = pair_tiles(n)
        slot_ref[0] = scores(j_a, True, 2 * n >= i - 1)
        slot_ref[1] = scores(j_b, True, 2 * n + 1 >= i - 1)

    def absorb(n, slot_ref, m_a, m_b):
        j_a, j_b = pair_tiles(n)
        m_a2, p_a = probs(slot_ref[0], m_a)
        m_b2, p_b = probs(slot_ref[1], m_b)
        acc_ref[0] = jnp.exp(m_a - m_a2) * acc_ref[0] + weighted(j_a, p_a)
        acc_ref[1] = jnp.exp(m_b - m_b2) * acc_ref[1] + weighted(j_b, p_b)
        return m_a2, m_b2

    stage_scores(0, s0_ref)

    def older(n2, c):
        m_a, m_b = c
        stage_scores(2 * n2 + 1, s1_ref)
        m_a, m_b = absorb(2 * n2, s0_ref, m_a, m_b)
        stage_scores(2 * n2 + 2, s0_ref)
        return absorb(2 * n2 + 1, s1_ref, m_a, m_b)

    m_a, m_b = lax.fori_loop(0, (i // 2 + 1) // 2, older, (m_a, m_b))
    m_run = jnp.maximum(m_a, m_b)
    acc = jnp.exp(m_a - m_run) * acc_ref[0] + jnp.exp(m_b - m_run) * acc_ref[1]
    denom = acc[LANES:LANES + 1, :]
    out_t = jnp.concatenate([acc[h * hd:(h + 1) * hd, h * t:(h + 1) * t] / denom[:, h * t:(h + 1) * t]
                             for h in range(HEAD_PAIR)], axis=0)
    o_ref[0] = out_t.T.astype(BF16)


def _bias_tab_kernel(rb_ref, bucket_ref, o_ref):
    h = pl.program_id(0)
    t = ATT_TILE
    far = rb_ref[h, REL_BUCKETS - 1]
    causal = lax.broadcasted_iota(I32, (t, t), 1) >= lax.broadcasted_iota(I32, (t, t), 0)
    for w in range(2):
        bk = bucket_ref[w]
        tile = jnp.zeros((t, t), F32)
        for b in range(REL_BUCKETS):
            tile = jnp.where(bk == b, rb_ref[h, b] - far, tile)
        o_ref[0, w] = jnp.where(causal, tile, NEG) if w == 0 else tile


def _moba_bias_table(rel_bias):
    assert MB_BLOCK >= REL_MAX_DIST and ATT_TILE == MB_BLOCK
    t = ATT_TILE
    nh = rel_bias.shape[0]
    d = jnp.arange(t, dtype=I32)[None, :] - jnp.arange(t, dtype=I32)[:, None]
    bucket = jnp.stack([_t5_bucket(d), _t5_bucket(d + t)]).astype(I32)
    return pl.pallas_call(
        _bias_tab_kernel,
        out_shape=jax.ShapeDtypeStruct((nh // HEAD_PAIR, 2, t, HEAD_PAIR * t), F32),
        grid=(nh,),
        in_specs=[pl.BlockSpec(memory_space=pltpu.SMEM),
                  pl.BlockSpec((2, t, t), lambda h: (0, 0, 0))],
        out_specs=pl.BlockSpec((1, 2, t, t), lambda h: (h // HEAD_PAIR, 0, 0, h % HEAD_PAIR)),
        compiler_params=_cparams(1),
        name="bias_tab",
    )(rel_bias, bucket)


def _blocked_t(a):
    b, s, w = a.shape
    return a.reshape(b, s // ATT_TILE, ATT_TILE, w).transpose(0, 1, 3, 2)


def _mb_attn(qkv, bias_tab, q_col, k_col, v_col):
    b, s, _ = qkv.shape
    t = ATT_TILE
    nb = s // t
    assert t == MB_BLOCK and s % t == 0 and nb <= HEAD_DIM and HEAD_PAIR == 2
    npair = MB_HEADS // HEAD_PAIR
    qt = _blocked_t(qkv[:, :, q_col * LANES:q_col * LANES + MB_WIDTH])
    vt = _blocked_t(qkv[:, :, v_col * LANES:v_col * LANES + MB_WIDTH])
    return pl.pallas_call(
        functools.partial(_mb_kernel, nb=nb),
        out_shape=jax.ShapeDtypeStruct((b, s, MB_WIDTH), BF16),
        grid=(b, npair, nb),
        in_specs=[pl.BlockSpec((1, 1, LANES, t), lambda bi, p, i: (bi, i, p, 0)),
                  pl.BlockSpec((1, s, LANES), lambda bi, p, i: (bi, 0, k_col + p)),
                  pl.BlockSpec((1, nb, LANES, t), lambda bi, p, i: (bi, 0, p, 0)),
                  pl.BlockSpec((1, 2, t, HEAD_PAIR * t), lambda bi, p, i: (p, 0, 0, 0))],
        out_specs=pl.BlockSpec((1, t, LANES), lambda bi, p, i: (bi, i, p)),
        scratch_shapes=[pltpu.VMEM((HEAD_DIM, LANES), F32),
                        pltpu.VMEM((2, t, HEAD_PAIR * t), F32),
                        pltpu.VMEM((2, t, HEAD_PAIR * t), F32),
                        pltpu.VMEM((2, LANES + ONES_ROWS, HEAD_PAIR * t), F32)],
        compiler_params=_cparams(3),
        name="mb_attn",
    )(qt, qkv, vt, bias_tab)


def _t5_bucket(dist):
    n = jnp.maximum(dist, 0)
    max_exact = REL_BUCKETS // 2
    nf = jnp.maximum(n, 1).astype(F32)
    large = max_exact + (jnp.log(nf / max_exact) / jnp.log(jnp.float32(REL_MAX_DIST / max_exact))
                         * (REL_BUCKETS - max_exact)).astype(I32)
    large = jnp.minimum(large, REL_BUCKETS - 1)
    return jnp.where(n < max_exact, n, large)


def _out_proj_kernel(osb_ref, omb_ref, gate_ref, x_ref, mod_ref, g2_ref, wsb_ref, wmb_ref, wout_ref,
                     x1_ref, h2_ref, h2p_ref):
    d = x_ref.shape[-1]
    gates = gate_ref[0].astype(F32)
    merged = (gates[:, :d] * _dot(osb_ref[0], wsb_ref[...])
              + gates[:, d:] * _dot(omb_ref[0], wmb_ref[...]))
    x1 = x_ref[0] + mod_ref[0, 2:3, :] * _dot(merged.astype(BF16), wout_ref[...])
    x1_ref[0] = x1
    h2 = _rms_mod(x1, g2_ref[...], mod_ref[0, 3:4, :], mod_ref[0, 4:5, :])
    h2_ref[0] = h2
    h2p_ref[0] = _pack_halves(h2)


def _out_proj(osb, omb, gates, x, mod, g2, wsb, wmb, wout, tm=512):
    b, s, d = x.shape
    row = lambda w: pl.BlockSpec((1, tm, w), lambda bi, i: (bi, i, 0))
    full = lambda a: pl.BlockSpec(a.shape, lambda bi, i: (0,) * a.ndim)
    return pl.pallas_call(
        _out_proj_kernel,
        out_shape=(jax.ShapeDtypeStruct((b, s, d), F32), jax.ShapeDtypeStruct((b, s, d), F32),
                   jax.ShapeDtypeStruct((b, s, d // 2), I32)),
        grid=(b, s // tm),
        in_specs=[row(osb.shape[-1]), row(omb.shape[-1]), row(gates.shape[-1]), row(d),
                  pl.BlockSpec((1, N_MOD, d), lambda bi, i: (bi, 0, 0)),
                  full(g2), full(wsb), full(wmb), full(wout)],
        out_specs=(row(d), row(d), row(d // 2)),
        compiler_params=_cparams(2),
        name="out_proj",
    )(osb, omb, gates, x, mod, g2, wsb, wmb, wout)


def _first_argmax(v, iota, n):
    m = jnp.max(v, axis=0, keepdims=True)
    idx = jnp.min(jnp.where(v == m, iota, n), axis=0, keepdims=True)
    return m, idx


def _route_kernel(h_ref, wr_ref, rb_ref, eidx_ref, wts_ref, slot_ref, cnt_ref, carry_ref):
    tm = h_ref.shape[0]
    e, g, gs = N_EXPERTS, N_GROUPS, GROUP_SIZE
    step = pl.program_id(0)

    @pl.when(step == 0)
    def _():
        carry_ref[...] = jnp.zeros_like(carry_ref)

    h0, h1, _ = _split3(h_ref[...])
    w0, w1, _ = _split3(wr_ref[...])
    logits = _dot_nt(w0, h0) + _dot_nt(w0, h1) + _dot_nt(w1, h0)
    scores = _sigmoid(logits)
    choice = scores + rb_ref[...]

    c3 = choice.reshape(g, gs, tm)
    sub = lax.broadcasted_iota(I32, (g, gs, tm), 1)
    m1 = jnp.max(c3, axis=1, keepdims=True)
    i1 = jnp.min(jnp.where(c3 == m1, sub, gs), axis=1, keepdims=True)
    m2 = jnp.max(jnp.where(sub == i1, -jnp.inf, c3), axis=1, keepdims=True)
    gscore = (m1 + m2).reshape(g, tm)

    giota = lax.broadcasted_iota(I32, (g, tm), 0)
    gmask = jnp.zeros((g, tm), F32)
    for _ in range(TOPK_GROUPS):
        _, gi = _first_argmax(gscore, giota, g)
        pick = giota == gi
        gmask = jnp.where(pick, 1.0, gmask)
        gscore = jnp.where(pick, -jnp.inf, gscore)
    emask = jnp.broadcast_to(gmask.reshape(g, 1, tm), (g, gs, tm)).reshape(e, tm)
    masked = jnp.where(emask > 0.5, choice, NEG)

    eiota = lax.broadcasted_iota(I32, (e, tm), 0)
    chosen = jnp.zeros((e, tm), F32)
    idxs, ws = [], []
    for _ in range(TOPK_EXPERTS):
        _, ei = _first_argmax(masked, eiota, e)
        pick = eiota == ei
        idxs.append(ei)
        ws.append(jnp.sum(jnp.where(pick, scores, 0.0), axis=0, keepdims=True))
        chosen = jnp.where(pick, 1.0, chosen)
        masked = jnp.where(pick, -jnp.inf, masked)
    wsum = ws[0]
    for w in ws[1:]:
        wsum = wsum + w

    r = lax.broadcasted_iota(I32, (tm, tm), 0)
    c = lax.broadcasted_iota(I32, (tm, tm), 1)
    before = jnp.where(r < c, 1.0, 0.0).astype(BF16)
    prefix = _dot(chosen.astype(BF16), before) + carry_ref[...]
    for k in range(TOPK_EXPERTS):
        eidx_ref[k:k + 1, :] = idxs[k]
        wts_ref[k:k + 1, :] = ws[k] / wsum * ROUTED_SCALE
        slot = jnp.sum(jnp.where(eiota == idxs[k], prefix, 0.0), axis=0, keepdims=True)
        slot_ref[k:k + 1, :] = slot.astype(I32)
    carry_ref[...] = carry_ref[...] + jnp.sum(chosen, axis=1, keepdims=True)
    cnt_ref[...] = carry_ref[...].astype(I32)


def _route(h2, w_router_t, router_bias, tm=512):
    t, d = h2.shape
    e = N_EXPERTS
    kk = TOPK_EXPERTS
    tok = lambda: pl.BlockSpec((kk, tm), lambda i: (0, i))
    return pl.pallas_call(
        _route_kernel,
        out_shape=(jax.ShapeDtypeStruct((kk, t), I32), jax.ShapeDtypeStruct((kk, t), F32),
                   jax.ShapeDtypeStruct((kk, t), I32), jax.ShapeDtypeStruct((e, 1), I32)),
        grid=(t // tm,),
        in_specs=[pl.BlockSpec((tm, d), lambda i: (i, 0)),
                  pl.BlockSpec((e, d), lambda i: (0, 0)),
                  pl.BlockSpec((e, 1), lambda i: (0, 0))],
        out_specs=(tok(), tok(), tok(), pl.BlockSpec((e, 1), lambda i: (0, 0))),
        scratch_shapes=[pltpu.VMEM((e, 1), F32)],
        compiler_params=_cparams(1),
        name="route",
    )(h2, w_router_t, router_bias.reshape(e, 1))


def _dest_kernel(eidx_ref, slot_ref, pstart_ref, o_ref):
    kk, tm = eidx_ref.shape
    eiota = lax.broadcasted_iota(I32, (N_EXPERTS, tm), 0)
    for k in range(kk):
        base = jnp.sum(jnp.where(eiota == eidx_ref[k:k + 1, :], pstart_ref[...], 0), axis=0, keepdims=True)
        o_ref[k:k + 1, :] = base + slot_ref[k:k + 1, :]


def _dest(eidx, slot, pstart, tm=2048):
    kk, t = eidx.shape
    tm = min(tm, t)
    tok = pl.BlockSpec((kk, tm), lambda i: (0, i))
    return pl.pallas_call(
        _dest_kernel,
        out_shape=jax.ShapeDtypeStruct((kk, t), I32),
        grid=(t // tm,),
        in_specs=[tok, tok, pl.BlockSpec((N_EXPERTS, 1), lambda i: (0, 0))],
        out_specs=tok,
        compiler_params=_cparams(1),
        name="dest",
    )(eidx, slot, pstart.reshape(N_EXPERTS, 1))


def _pack_halves(x):
    n = x.shape[1] // 2
    hi = lax.bitcast_convert_type(x[:, :n].astype(BF16).astype(F32), I32)
    lo = lax.bitcast_convert_type(x[:, n:].astype(BF16).astype(F32), I32)
    return hi | lax.shift_right_logical(lo, jnp.full_like(lo, 16))


def _unpack_halves(p):
    hi = lax.bitcast_convert_type(p & jnp.int32(-65536), F32)
    lo = lax.bitcast_convert_type(p << 16, F32)
    return hi, lo


def _sc_gather_rows(table, idx):
    n = idx.shape[0]
    w = table.shape[1]
    workers = SC_CORES * SC_SUBCORES
    assert n % (workers * SC_WINDOW * 2) == 0
    per_worker = n // workers
    n_win = per_worker // SC_WINDOW
    mesh = plsc.VectorSubcoreMesh(core_axis_name="c", subcore_axis_name="s")

    @functools.partial(
        pl.kernel, mesh=mesh,
        out_type=jax.ShapeDtypeStruct((n, w), table.dtype),
        scratch_types=[pltpu.VMEM((SC_WINDOW,), I32), pltpu.VMEM((SC_WINDOW,), I32),
                       pltpu.VMEM((SC_WINDOW, w), table.dtype), pltpu.VMEM((SC_WINDOW, w), table.dtype),
                       pltpu.SemaphoreType.DMA, pltpu.SemaphoreType.DMA],
    )
    def gather(table_hbm, idx_hbm, out_hbm, idx0, idx1, rows0, rows1, sem0, sem1):
        wid = lax.axis_index("s") * SC_CORES + lax.axis_index("c")

        def start(g, idx_v, rows_v, sem):
            pltpu.sync_copy(idx_hbm.at[pl.ds(wid * per_worker + g * SC_WINDOW, SC_WINDOW)], idx_v)
            pltpu.make_async_copy(table_hbm.at[idx_v], rows_v, sem).start()

        def finish(g, idx_v, rows_v, sem):
            pltpu.make_async_copy(table_hbm.at[idx_v], rows_v, sem).wait()
            pltpu.sync_copy(rows_v, out_hbm.at[pl.ds(wid * per_worker + g * SC_WINDOW, SC_WINDOW)])

        start(0, idx0, rows0, sem0)

        @pl.loop(0, n_win // 2)
        def _(h):
            g = 2 * h
            start(g + 1, idx1, rows1, sem1)
            finish(g, idx0, rows0, sem0)

            @pl.when(g + 2 < n_win)
            def _():
                start(g + 2, idx0, rows0, sem0)

            finish(g + 1, idx1, rows1, sem1)

    return gather(table, idx)


def _sc_scatter_rows(rows, idx, n_out):
    t, w = rows.shape
    kk = idx.shape[0]
    win = LANES
    workers = SC_CORES * SC_SUBCORES
    assert t % (workers * win) == 0
    per_worker = t // workers
    mesh = plsc.VectorSubcoreMesh(core_axis_name="c", subcore_axis_name="s")

    @functools.partial(
        pl.kernel, mesh=mesh,
        out_type=jax.ShapeDtypeStruct((n_out, w), rows.dtype),
        scratch_types=[pltpu.VMEM((kk, win), I32), pltpu.VMEM((win, w), rows.dtype),
                       pltpu.SemaphoreType.DMA],
    )
    def scatter(rows_hbm, idx_hbm, out_hbm, idx_v, rows_v, sem):
        wid = lax.axis_index("s") * SC_CORES + lax.axis_index("c")

        @pl.loop(0, per_worker // win)
        def _(g):
            base = wid * per_worker + g * win
            pltpu.sync_copy(idx_hbm.at[:, pl.ds(base, win)], idx_v)
            pltpu.sync_copy(rows_hbm.at[pl.ds(base, win)], rows_v)
            for k in range(kk):
                pltpu.async_copy(rows_v, out_hbm.at[idx_v.at[k]], sem).wait()

    return scatter(rows, idx)


def _ffn_kernel(be_ref, valid_ref, xs_ref, wg_ref, wu_ref, wd_ref, ys_ref, wgu_ref, wdn_ref):
    i = pl.program_id(0)
    ff = wg_ref.shape[-1]
    rb, half = xs_ref.shape

    @pl.when((i == 0) | (be_ref[i] != be_ref[jnp.maximum(i - 1, 0)]))
    def _():
        wgu_ref[:, :ff] = wg_ref[0].astype(BF16)
        wgu_ref[:, ff:] = wu_ref[0].astype(BF16)
        wdn_ref[...] = wd_ref[0].astype(BF16)

    @pl.when(valid_ref[i] > 0)
    def _():
        is_token = lax.broadcasted_iota(I32, (rb, half), 0) < valid_ref[i]
        x_hi, x_lo = _unpack_halves(jnp.where(is_token, xs_ref[...], jnp.zeros((rb, half), xs_ref.dtype)))
        au = _dot(x_hi.astype(BF16), wgu_ref[:half, :]) + _dot(x_lo.astype(BF16), wgu_ref[half:, :])
        a, u = au[:, :ff], au[:, ff:]
        act = (a * _sigmoid(a) * u).astype(BF16)
        ys_ref[...] = _pack_halves(_dot(act, wdn_ref[...]))

    @pl.when(valid_ref[i] <= 0)
    def _():
        ys_ref[...] = jnp.zeros_like(ys_ref)


def _ffn(blk_expert, blk_valid, xs, wg, wu, wd):
    n_rows, w = xs.shape
    rb = ROW_BLOCK
    _, d, ff = wg.shape
    return pl.pallas_call(
        _ffn_kernel,
        out_shape=jax.ShapeDtypeStruct((n_rows, w), I32),
        grid_spec=pltpu.PrefetchScalarGridSpec(
            num_scalar_prefetch=2,
            grid=(n_rows // rb,),
            in_specs=[pl.BlockSpec((rb, w), lambda i, be, bv: (i, 0)),
                      pl.BlockSpec((1, d, ff), lambda i, be, bv: (be[i], 0, 0)),
                      pl.BlockSpec((1, d, ff), lambda i, be, bv: (be[i], 0, 0)),
                      pl.BlockSpec((1, ff, d), lambda i, be, bv: (be[i], 0, 0))],
            out_specs=pl.BlockSpec((rb, w), lambda i, be, bv: (i, 0)),
            scratch_shapes=[pltpu.VMEM((d, 2 * ff), BF16), pltpu.VMEM((ff, d), BF16)],
        ),
        compiler_params=_cparams(1),
        name="ffn",
    )(blk_expert, blk_valid, xs, wg, wu, wd)


def _combine_kernel(rows_ref, wts_ref, h_ref, x1_ref, gate2_ref, fg_ref, wgus_ref, wds_ref, o_ref):
    half = h_ref.shape[-1]
    ff = wds_ref.shape[0]
    h_hi, h_lo = _unpack_halves(h_ref[0])
    au = _dot(h_hi.astype(BF16), wgus_ref[:half, :]) + _dot(h_lo.astype(BF16), wgus_ref[half:, :])
    a, u = au[:, :ff], au[:, ff:]
    y = _dot((a * _sigmoid(a) * u).astype(BF16), wds_ref[...])

    w = wts_ref[0]
    y_hi, y_lo = y[:, :half], y[:, half:]
    for k in range(TOPK_EXPERTS):
        r_hi, r_lo = _unpack_halves(rows_ref[k, 0])
        y_hi = y_hi + w[:, k:k + 1] * r_hi
        y_lo = y_lo + w[:, k:k + 1] * r_lo
    y = jnp.concatenate([y_hi, y_lo], axis=1)
    x2 = x1_ref[0] + gate2_ref[0] * y
    o_ref[0] = x2 * lax.rsqrt(jnp.mean(x2 * x2, axis=-1, keepdims=True) + EPS) * fg_ref[...]


def _combine(rows, wts, h2p, x1, gate2, final_g, wgus, wds, tb=256):
    b, s, d = x1.shape
    kk = rows.shape[0]
    w = h2p.shape[-1]
    row = lambda n: pl.BlockSpec((1, tb, n), lambda bi, i: (bi, i, 0))
    full = lambda a: pl.BlockSpec(a.shape, lambda bi, i: (0,) * a.ndim)
    return pl.pallas_call(
        _combine_kernel,
        out_shape=jax.ShapeDtypeStruct((b, s, d), F32),
        grid=(b, s // tb),
        in_specs=[pl.BlockSpec((kk, 1, tb, w), lambda bi, i: (0, bi, i, 0)),
                  row(kk), row(w), row(d),
                  pl.BlockSpec((1, 1, d), lambda bi, i: (bi, 0, 0)),
                  full(final_g), full(wgus), full(wds)],
        out_specs=row(d),
        compiler_params=_cparams(2),
        name="combine",
    )(rows, wts, h2p, x1, gate2, final_g, wgus, wds)


def _layer(x, mod, norm1_g, norm2_g, w_in, w_branch_sb, w_branch_mb, w_out, w_router, router_bias,
           w_gate_e, w_up_e, w_down_e, w_gate_sh, w_up_sh, w_down_sh, bias_tab, final_g):
    b, s, d = x.shape
    t = b * s
    nqkv = 3 * SB_WIDTH + 3 * MB_WIDTH

    col = jnp.arange(nqkv)
    is_q = (col < SB_WIDTH) | ((col >= 3 * SB_WIDTH) & (col < 3 * SB_WIDTH + MB_WIDTH))
    wqkv = (w_in[:, :nqkv] * jnp.where(is_q, HEAD_DIM ** -0.5, 1.0)).astype(BF16)
    wg = w_in[:, nqkv:].astype(BF16)
    qkv, gates = _in_proj(x, mod, norm1_g.reshape(1, d), wqkv, wg)

    cb = SB_WIDTH // LANES
    o_sb = _sb_attn(qkv, 0, cb, 2 * cb)
    o_mb = _mb_attn(qkv, bias_tab, 3 * cb, 4 * cb, 5 * cb)

    x1, h2, h2p = _out_proj(o_sb, o_mb, gates, x, mod, norm2_g.reshape(1, d), w_branch_sb.astype(BF16),
                            w_branch_mb.astype(BF16), w_out.astype(BF16))

    eidx, wts, slot, counts = _route(h2.reshape(t, d), w_router.T, router_bias)

    rb = ROW_BLOCK
    counts = counts.reshape(N_EXPERTS)
    pcounts = (counts + rb - 1) // rb * rb
    pend = jnp.cumsum(pcounts)
    pstart = pend - pcounts
    n_blk = t * TOPK_EXPERTS // rb + N_EXPERTS
    blk_row = jnp.arange(n_blk, dtype=I32) * rb
    blk_expert = jnp.minimum(jnp.sum(pend[None, :] <= blk_row[:, None], axis=1), N_EXPERTS - 1).astype(I32)
    blk_valid = jnp.clip((pstart + counts)[blk_expert] - blk_row, 0, rb).astype(I32)
    dest = _dest(eidx, slot, pstart.astype(I32))

    xs = _sc_scatter_rows(h2p.reshape(t, d // 2), dest, n_blk * rb)
    ys = _ffn(blk_expert, blk_valid, xs, w_gate_e, w_up_e, w_down_e)
    rows = _sc_gather_rows(ys, dest.reshape(TOPK_EXPERTS * t)).reshape(TOPK_EXPERTS, b, s, d // 2)
    wts_tok = wts.T.reshape(b, s, TOPK_EXPERTS)
    wgus = jnp.concatenate([w_gate_sh, w_up_sh], axis=1).astype(BF16)
    return _combine(rows, wts_tok, h2p, x1, mod[:, 5:6, :], final_g.reshape(1, d), wgus, w_down_sh.astype(BF16))


def kernel(x, c, norm1_g, norm2_g, w_ada, b_ada, w_in, w_branch_sb, w_branch_mb, w_out, w_router, router_bias,
           w_gate_e, w_up_e, w_down_e, w_gate_sh, w_up_sh, w_down_sh, rel_bias, final_g):
    b, s, d = x.shape
    depth = w_ada.shape[0]
    assert depth == 1, "the final norm is fused into the only layer's combine step"
    bias_tab = _moba_bias_table(rel_bias)
    l = 0
    mod = _ada(c, w_ada[l], b_ada[l]).reshape(b, N_MOD, d)
    return _layer(x, mod, norm1_g[l], norm2_g[l], w_in[l], w_branch_sb[l], w_branch_mb[l], w_out[l],
                  w_router[l], router_bias[l], w_gate_e[l], w_up_e[l], w_down_e[l], w_gate_sh[l],
                  w_up_sh[l], w_down_sh[l], bias_tab, final_g)
```

```python
import functools

import jax
import jax.numpy as jnp
from jax import lax
from jax.experimental import pallas as pl
from jax.experimental.pallas import tpu as pltpu
from jax.experimental.pallas import tpu_sc as plsc

F32 = jnp.float32
BF16 = jnp.bfloat16
I32 = jnp.int32

HEAD_DIM = 64
SB_HEADS = 8
MB_HEADS = 8
SB_WIDTH = SB_HEADS * HEAD_DIM
MB_WIDTH = MB_HEADS * HEAD_DIM
MB_BLOCK = 256
MB_TOPK = 3
REL_BUCKETS = 32
REL_MAX_DIST = 128
N_EXPERTS = 64
N_GROUPS = 8
GROUP_SIZE = N_EXPERTS // N_GROUPS
TOPK_GROUPS = 4
TOPK_EXPERTS = 8
ROUTED_SCALE = 2.5
N_MOD = 6
EPS = 1e-6
NEG = -1e30

LANES = 128
HEAD_PAIR = LANES // HEAD_DIM
ATT_TILE = 256
SB_CUTOFF = 120.0
ONES_ROWS = 16
ROW_BLOCK = 512

SC_CORES = 2
SC_SUBCORES = 16
SC_WINDOW = 64
VMEM_LIMIT = 56 * 1024 * 1024


def _cparams(n_axes):
    return pltpu.CompilerParams(dimension_semantics=("arbitrary",) * n_axes,
                                vmem_limit_bytes=VMEM_LIMIT)


def _dot(a, b):
    return jnp.dot(a, b, preferred_element_type=F32)


def _dot_nt(a, b):
    return lax.dot_general(a, b, (((1,), (1,)), ((), ())), preferred_element_type=F32)


def _split3(a):
    p0 = a.astype(BF16)
    r0 = a - p0.astype(F32)
    p1 = r0.astype(BF16)
    p2 = (r0 - p1.astype(F32)).astype(BF16)
    return p0, p1, p2


def _sigmoid(x):
    return 1.0 / (1.0 + jnp.exp(-x))


def _rms_mod(x, g, shift, scale):
    y = x * lax.rsqrt(jnp.mean(x * x, axis=-1, keepdims=True) + EPS) * g
    return y * (1.0 + scale) + shift


def _ada_kernel(c_ref, w_ref, b_ref, o_ref):
    c = c_ref[...]
    s = c * _sigmoid(c)
    s0, s1, _ = _split3(s)
    w0, w1, _ = _split3(w_ref[...])
    o_ref[...] = _dot(s0, w0) + _dot(s0, w1) + _dot(s1, w0) + b_ref[...]


def _ada(c, w_ada, b_ada):
    b, d = c.shape
    n = w_ada.shape[1]
    tn = 1536
    return pl.pallas_call(
        _ada_kernel,
        out_shape=jax.ShapeDtypeStruct((b, n), F32),
        grid=(n // tn,),
        in_specs=[pl.BlockSpec((b, d), lambda j: (0, 0)),
                  pl.BlockSpec((d, tn), lambda j: (0, j)),
                  pl.BlockSpec((1, tn), lambda j: (0, j))],
        out_specs=pl.BlockSpec((b, tn), lambda j: (0, j)),
        compiler_params=_cparams(1),
        name="ada",
    )(c, w_ada, b_ada.reshape(1, n))


def _in_proj_kernel(x_ref, mod_ref, g_ref, wqkv_ref, wg_ref, qkv_ref, gate_ref):
    h = _rms_mod(x_ref[0], g_ref[...], mod_ref[0, 0:1, :], mod_ref[0, 1:2, :]).astype(BF16)
    qkv_ref[0] = _dot(h, wqkv_ref[...]).astype(BF16)
    gate_ref[0] = _sigmoid(_dot(h, wg_ref[...])).astype(BF16)


def _in_proj(x, mod, g, wqkv, wg, tm=512):
    b, s, d = x.shape
    nq, ng = wqkv.shape[1], wg.shape[1]
    return pl.pallas_call(
        _in_proj_kernel,
        out_shape=(jax.ShapeDtypeStruct((b, s, nq), BF16), jax.ShapeDtypeStruct((b, s, ng), BF16)),
        grid=(b, s // tm),
        in_specs=[pl.BlockSpec((1, tm, d), lambda bi, i: (bi, i, 0)),
                  pl.BlockSpec((1, N_MOD, d), lambda bi, i: (bi, 0, 0)),
                  pl.BlockSpec((1, d), lambda bi, i: (0, 0)),
                  pl.BlockSpec((d, nq), lambda bi, i: (0, 0)),
                  pl.BlockSpec((d, ng), lambda bi, i: (0, 0))],
        out_specs=(pl.BlockSpec((1, tm, nq), lambda bi, i: (bi, i, 0)),
                   pl.BlockSpec((1, tm, ng), lambda bi, i: (bi, i, 0))),
        compiler_params=_cparams(2),
        name="in_proj",
    )(x, mod, g, wqkv, wg)


def _head_mask(h):
    lane = lax.broadcasted_iota(I32, (ATT_TILE, LANES), 1)
    return (lane >= h * HEAD_DIM) & (lane < (h + 1) * HEAD_DIM)


def _sb_kernel(q_ref, k_ref, v_ref, o_ref):
    t = ATT_TILE
    rows = HEAD_PAIR * t
    i = pl.program_id(2)
    row = lax.broadcasted_iota(I32, (t, t), 0)
    col = lax.broadcasted_iota(I32, (t, t), 1)
    later = jnp.where(row > col, 1.0, 0.0).astype(BF16)
    past = (lax.broadcasted_iota(I32, (rows, t), 1)
            < lax.broadcasted_iota(I32, (rows, t), 0) % t)
    q = q_ref[0]
    q2 = jnp.concatenate([jnp.where(_head_mask(h), q, jnp.zeros_like(q)) for h in range(HEAD_PAIR)], axis=0)

    def local(j, diag):
        start = pl.multiple_of(j * t, t)
        kj = k_ref[0, pl.ds(start, t), :]
        z = _dot_nt(q2, kj)
        sp = jnp.maximum(z, 0.0) + jnp.log(1.0 + jnp.exp(-jnp.abs(z)))
        log_rest = -sp
        if diag:
            log_rest = jnp.where(past, log_rest, 0.0)
        hi = log_rest.astype(BF16)
        lo = (log_rest - hi.astype(F32)).astype(BF16)
        cum = _dot(jnp.concatenate([hi, lo], axis=0), later)
        return z - sp, log_rest, cum[:rows] + cum[rows:], v_ref[0, pl.ds(start, t), :]

    has_prev = i > 0
    lb_d, rest_d, skip_d, v_d = local(i, True)
    lb_p, rest_p, skip_p, v_p = local(jnp.maximum(i - 1, 0), False)
    a_d = jnp.where(past, jnp.exp(lb_d + skip_d), 0.0)
    carry = jnp.sum(rest_d, axis=1, keepdims=True)
    a_p = jnp.where(has_prev, jnp.exp(lb_p + skip_p + carry), 0.0)
    acc = _dot(a_d.astype(BF16), v_d) + _dot(a_p.astype(BF16), v_p)
    carry = carry + jnp.where(has_prev, jnp.sum(rest_p, axis=1, keepdims=True), 0.0)

    def live(st):
        j, carry, _ = st
        return (j >= 0) & (jnp.max(carry) > -SB_CUTOFF)

    def older(st):
        j, carry, acc = st
        lb, rest, skip, vj = local(j, False)
        a = jnp.exp(lb + skip + carry)
        return j - 1, carry + jnp.sum(rest, axis=1, keepdims=True), acc + _dot(a.astype(BF16), vj)

    _, _, acc = lax.while_loop(live, older, (i - 2, carry, acc))
    o_ref[0] = jnp.where(_head_mask(0), acc[:t], acc[t:]).astype(BF16)


def _sb_attn(qkv, q_col, k_col, v_col):
    b, s, _ = qkv.shape
    t = ATT_TILE
    npair = SB_HEADS // HEAD_PAIR
    return pl.pallas_call(
        _sb_kernel,
        out_shape=jax.ShapeDtypeStruct((b, s, SB_WIDTH), BF16),
        grid=(b, npair, s // t),
        in_specs=[pl.BlockSpec((1, t, LANES), lambda bi, p, i: (bi, i, q_col + p)),
                  pl.BlockSpec((1, s, LANES), lambda bi, p, i: (bi, 0, k_col + p)),
                  pl.BlockSpec((1, s, LANES), lambda bi, p, i: (bi, 0, v_col + p))],
        out_specs=pl.BlockSpec((1, t, LANES), lambda bi, p, i: (bi, i, p)),
        compiler_params=_cparams(3),
        name="sb_attn",
    )(qkv, qkv, qkv)


def _mb_kernel(qt_ref, k_ref, vt_ref, bias_ref, o_ref, kbar_ref, s0_ref, s1_ref, acc_ref, *, nb):
    t = ATT_TILE
    hd = HEAD_DIM
    i = pl.program_id(2)
    lane_row = lax.broadcasted_iota(I32, (1, LANES), 1)
    slot = lax.broadcasted_iota(I32, (LANES, t), 0)
    blk = lax.broadcasted_iota(I32, (hd, t), 0)

    @pl.when(i == 0)
    def _():
        kbar_ref[...] = jnp.zeros_like(kbar_ref)
        for n in range(nb):
            kbar_ref[n:n + 1, :] = jnp.mean(k_ref[0, n * t:(n + 1) * t, :].astype(F32), axis=0, keepdims=True)

    qt = qt_ref[0, 0]
    kb0, kb1, kb2 = _split3(kbar_ref[...])
    q_heads, not_sels = [], []
    for h in range(HEAD_PAIR):
        qh = jnp.where((slot >= h * hd) & (slot < (h + 1) * hd), qt, jnp.zeros_like(qt))
        gate = _dot(kb0, qh) + _dot(kb1, qh) + _dot(kb2, qh)
        gate = jnp.where(blk < i, gate, -jnp.inf)
        not_sel = jnp.ones((hd, t), F32)
        for _ in range(MB_TOPK):
            m = jnp.max(gate, axis=0, keepdims=True)
            idx = jnp.min(jnp.where(gate == m, blk, hd), axis=0, keepdims=True)
            pick = (blk == idx) & (m > -jnp.inf)
            not_sel = jnp.where(pick, 0.0, not_sel)
            gate = jnp.where(pick, -jnp.inf, gate)
        q_heads.append(qh)
        not_sels.append(not_sel.astype(BF16))
    q_ext = jnp.concatenate([jnp.concatenate(q_heads, axis=1), jnp.concatenate(not_sels, axis=1),
                             jnp.ones((ONES_ROWS, HEAD_PAIR * t), BF16),
                             jnp.zeros((hd - ONES_ROWS, HEAD_PAIR * t), BF16)], axis=0)
    ones_rows = jnp.ones((ONES_ROWS, t), BF16)

    def scores(j, masked, dead=None):
        kj = k_ref[0, pl.ds(pl.multiple_of(j * t, t), t), :]
        pen = jnp.where(lane_row == j, NEG, 0.0) if masked else jnp.zeros((1, LANES), F32)
        if dead is not None:
            pen = pen + jnp.where(lane_row == hd, jnp.where(dead, NEG, 0.0), 0.0)
        return _dot(jnp.concatenate([kj, jnp.broadcast_to(pen.astype(BF16), (t, LANES))], axis=1), q_ext)

    def weighted(j, p):
        return _dot(jnp.concatenate([vt_ref[0, j], ones_rows], axis=0), p)

    def probs(s, m_run):
        m_new = jnp.maximum(m_run, jnp.max(s, axis=0, keepdims=True))
        return m_new, jnp.exp(s - m_new).astype(BF16)

    i_prev = jnp.maximum(i - 1, 0)
    s_a = scores(i, False) + bias_ref[0, 0]
    s_b = scores(i_prev, True) + bias_ref[0, 1]

    last_older = jnp.maximum(i - 2, 0)

    def pair_tiles(n):
        return jnp.minimum(2 * n, last_older), jnp.minimum(2 * n + 1, last_older)

    def stage_scores(n, slot_ref):
        j_a, j_b = pair_tiles(n)
        slot_ref[0] = scores(j_a, True, 2 * n >= i - 1)
        slot_ref[1] = scores(j_b, True, 2 * n + 1 >= i - 1)

    def absorb(n, slot_ref, m_a, m_b):
        j_a, j_b = pair_tiles(n)
        m_a2, p_a = probs(slot_ref[0], m_a)
        m_b2, p_b = probs(slot_ref[1], m_b)
        acc_ref[0] = jnp.exp(m_a - m_a2) * acc_ref[0] + weighted(j_a, p_a)
        acc_ref[1] = jnp.exp(m_b - m_b2) * acc_ref[1] + weighted(j_b, p_b)
        return m_a2, m_b2

    stage_scores(0, s0_ref)
    m_a, p_a = probs(s_a, jnp.full((1, HEAD_PAIR * t), NEG, F32))
    m_b, p_b = probs(s_b, jnp.full((1, HEAD_PAIR * t), NEG, F32))
    acc_ref[0] = weighted(i, p_a)
    acc_ref[1] = weighted(i_prev, p_b)

    def older(n2, c):
        m_a, m_b = c
        stage_scores(2 * n2 + 1, s1_ref)
        m_a, m_b = absorb(2 * n2, s0_ref, m_a, m_b)
        stage_scores(2 * n2 + 2, s0_ref)
        return absorb(2 * n2 + 1, s1_ref, m_a, m_b)

    m_a, m_b = lax.fori_loop(0, (i // 2 + 1) // 2, older, (m_a, m_b))
    m_run = jnp.maximum(m_a, m_b)
    acc = jnp.exp(m_a - m_run) * acc_ref[0] + jnp.exp(m_b - m_run) * acc_ref[1]
    denom = acc[LANES:LANES + 1, :]
    out_t = jnp.concatenate([acc[h * hd:(h + 1) * hd, h * t:(h + 1) * t] / denom[:, h * t:(h + 1) * t]
                             for h in range(HEAD_PAIR)], axis=0)
    o_ref[0] = out_t.T.astype(BF16)


def _bias_tab_kernel(rb_ref, bucket_ref, o_ref):
    h = pl.program_id(0)
    t = ATT_TILE
    far = rb_ref[h, REL_BUCKETS - 1]
    causal = lax.broadcasted_iota(I32, (t, t), 1) >= lax.broadcasted_iota(I32, (t, t), 0)
    for w in range(2):
        bk = bucket_ref[w]
        tile = jnp.zeros((t, t), F32)
        for b in range(REL_BUCKETS):
            tile = jnp.where(bk == b, rb_ref[h, b] - far, tile)
        o_ref[0, w] = jnp.where(causal, tile, NEG) if w == 0 else tile


def _moba_bias_table(rel_bias):
    assert MB_BLOCK >= REL_MAX_DIST and ATT_TILE == MB_BLOCK
    t = ATT_TILE
    nh = rel_bias.shape[0]
    d = jnp.arange(t, dtype=I32)[None, :] - jnp.arange(t, dtype=I32)[:, None]
    bucket = jnp.stack([_t5_bucket(d), _t5_bucket(d + t)]).astype(I32)
    return pl.pallas_call(
        _bias_tab_kernel,
        out_shape=jax.ShapeDtypeStruct((nh // HEAD_PAIR, 2, t, HEAD_PAIR * t), F32),
        grid=(nh,),
        in_specs=[pl.BlockSpec(memory_space=pltpu.SMEM),
                  pl.BlockSpec((2, t, t), lambda h: (0, 0, 0))],
        out_specs=pl.BlockSpec((1, 2, t, t), lambda h: (h // HEAD_PAIR, 0, 0, h % HEAD_PAIR)),
        compiler_params=_cparams(1),
        name="bias_tab",
    )(rel_bias, bucket)


def _blocked_t(a):
    b, s, w = a.shape
    return a.reshape(b, s // ATT_TILE, ATT_TILE, w).transpose(0, 1, 3, 2)


def _mb_attn(qkv, bias_tab, q_col, k_col, v_col):
    b, s, _ = qkv.shape
    t = ATT_TILE
    nb = s // t
    assert t == MB_BLOCK and s % t == 0 and nb <= HEAD_DIM and HEAD_PAIR == 2
    npair = MB_HEADS // HEAD_PAIR
    qt = _blocked_t(qkv[:, :, q_col * LANES:q_col * LANES + MB_WIDTH])
    vt = _blocked_t(qkv[:, :, v_col * LANES:v_col * LANES + MB_WIDTH])
    return pl.pallas_call(
        functools.partial(_mb_kernel, nb=nb),
        out_shape=jax.ShapeDtypeStruct((b, s, MB_WIDTH), BF16),
        grid=(b, npair, nb),
        in_specs=[pl.BlockSpec((1, 1, LANES, t), lambda bi, p, i: (bi, i, p, 0)),
                  pl.BlockSpec((1, s, LANES), lambda bi, p, i: (bi, 0, k_col + p)),
                  pl.BlockSpec((1, nb, LANES, t), lambda bi, p, i: (bi, 0, p, 0)),
                  pl.BlockSpec((1, 2, t, HEAD_PAIR * t), lambda bi, p, i: (p, 0, 0, 0))],
        out_specs=pl.BlockSpec((1, t, LANES), lambda bi, p, i: (bi, i, p)),
        scratch_shapes=[pltpu.VMEM((HEAD_DIM, LANES), F32),
                        pltpu.VMEM((2, t, HEAD_PAIR * t), F32),
                        pltpu.VMEM((2, t, HEAD_PAIR * t), F32),
                        pltpu.VMEM((2, LANES + ONES_ROWS, HEAD_PAIR * t), F32)],
        compiler_params=_cparams(3),
        name="mb_attn",
    )(qt, qkv, vt, bias_tab)


def _t5_bucket(dist):
    n = jnp.maximum(dist, 0)
    max_exact = REL_BUCKETS // 2
    nf = jnp.maximum(n, 1).astype(F32)
    large = max_exact + (jnp.log(nf / max_exact) / jnp.log(jnp.float32(REL_MAX_DIST / max_exact))
                         * (REL_BUCKETS - max_exact)).astype(I32)
    large = jnp.minimum(large, REL_BUCKETS - 1)
    return jnp.where(n < max_exact, n, large)


def _out_proj_kernel(osb_ref, omb_ref, gate_ref, x_ref, mod_ref, g2_ref, wsb_ref, wmb_ref, wout_ref,
                     x1_ref, h2_ref, h2p_ref):
    d = x_ref.shape[-1]
    gates = gate_ref[0].astype(F32)
    merged = (gates[:, :d] * _dot(osb_ref[0], wsb_ref[...])
              + gates[:, d:] * _dot(omb_ref[0], wmb_ref[...]))
    x1 = x_ref[0] + mod_ref[0, 2:3, :] * _dot(merged.astype(BF16), wout_ref[...])
    x1_ref[0] = x1
    h2 = _rms_mod(x1, g2_ref[...], mod_ref[0, 3:4, :], mod_ref[0, 4:5, :])
    h2_ref[0] = h2
    h2p_ref[0] = _pack_halves(h2)


def _out_proj(osb, omb, gates, x, mod, g2, wsb, wmb, wout, tm=512):
    b, s, d = x.shape
    row = lambda w: pl.BlockSpec((1, tm, w), lambda bi, i: (bi, i, 0))
    full = lambda a: pl.BlockSpec(a.shape, lambda bi, i: (0,) * a.ndim)
    return pl.pallas_call(
        _out_proj_kernel,
        out_shape=(jax.ShapeDtypeStruct((b, s, d), F32), jax.ShapeDtypeStruct((b, s, d), F32),
                   jax.ShapeDtypeStruct((b, s, d // 2), I32)),
        grid=(b, s // tm),
        in_specs=[row(osb.shape[-1]), row(omb.shape[-1]), row(gates.shape[-1]), row(d),
                  pl.BlockSpec((1, N_MOD, d), lambda bi, i: (bi, 0, 0)),
                  full(g2), full(wsb), full(wmb), full(wout)],
        out_specs=(row(d), row(d), row(d // 2)),
        compiler_params=_cparams(2),
        name="out_proj",
    )(osb, omb, gates, x, mod, g2, wsb, wmb, wout)


def _first_argmax(v, iota, n):
    m = jnp.max(v, axis=0, keepdims=True)
    idx = jnp.min(jnp.where(v == m, iota, n), axis=0, keepdims=True)
    return m, idx


def _route_kernel(h_ref, wr_ref, rb_ref, eidx_ref, wts_ref, slot_ref, cnt_ref, carry_ref):
    tm = h_ref.shape[0]
    e, g, gs = N_EXPERTS, N_GROUPS, GROUP_SIZE
    step = pl.program_id(0)

    @pl.when(step == 0)
    def _():
        carry_ref[...] = jnp.zeros_like(carry_ref)

    h0, h1, _ = _split3(h_ref[...])
    w0, w1, _ = _split3(wr_ref[...])
    logits = _dot_nt(w0, h0) + _dot_nt(w0, h1) + _dot_nt(w1, h0)
    scores = _sigmoid(logits)
    choice = scores + rb_ref[...]

    c3 = choice.reshape(g, gs, tm)
    sub = lax.broadcasted_iota(I32, (g, gs, tm), 1)
    m1 = jnp.max(c3, axis=1, keepdims=True)
    i1 = jnp.min(jnp.where(c3 == m1, sub, gs), axis=1, keepdims=True)
    m2 = jnp.max(jnp.where(sub == i1, -jnp.inf, c3), axis=1, keepdims=True)
    gscore = (m1 + m2).reshape(g, tm)

    giota = lax.broadcasted_iota(I32, (g, tm), 0)
    gmask = jnp.zeros((g, tm), F32)
    for _ in range(TOPK_GROUPS):
        _, gi = _first_argmax(gscore, giota, g)
        pick = giota == gi
        gmask = jnp.where(pick, 1.0, gmask)
        gscore = jnp.where(pick, -jnp.inf, gscore)
    emask = jnp.broadcast_to(gmask.reshape(g, 1, tm), (g, gs, tm)).reshape(e, tm)
    masked = jnp.where(emask > 0.5, choice, NEG)

    eiota = lax.broadcasted_iota(I32, (e, tm), 0)
    chosen = jnp.zeros((e, tm), F32)
    idxs, ws = [], []
    for _ in range(TOPK_EXPERTS):
        _, ei = _first_argmax(masked, eiota, e)
        pick = eiota == ei
        idxs.append(ei)
        ws.append(jnp.sum(jnp.where(pick, scores, 0.0), axis=0, keepdims=True))
        chosen = jnp.where(pick, 1.0, chosen)
        masked = jnp.where(pick, -jnp.inf, masked)
    wsum = ws[0]
    for w in ws[1:]:
        wsum = wsum + w

    r = lax.broadcasted_iota(I32, (tm, tm), 0)
    c = lax.broadcasted_iota(I32, (tm, tm), 1)
    before = jnp.where(r < c, 1.0, 0.0).astype(BF16)
    prefix = _dot(chosen.astype(BF16), before) + carry_ref[...]
    for k in range(TOPK_EXPERTS):
        eidx_ref[k:k + 1, :] = idxs[k]
        wts_ref[k:k + 1, :] = ws[k] / wsum * ROUTED_SCALE
        slot = jnp.sum(jnp.where(eiota == idxs[k], prefix, 0.0), axis=0, keepdims=True)
        slot_ref[k:k + 1, :] = slot.astype(I32)
    carry_ref[...] = carry_ref[...] + jnp.sum(chosen, axis=1, keepdims=True)
    cnt_ref[...] = carry_ref[...].astype(I32)


def _route(h2, w_router_t, router_bias, tm=512):
    t, d = h2.shape
    e = N_EXPERTS
    kk = TOPK_EXPERTS
    tok = lambda: pl.BlockSpec((kk, tm), lambda i: (0, i))
    return pl.pallas_call(
        _route_kernel,
        out_shape=(jax.ShapeDtypeStruct((kk, t), I32), jax.ShapeDtypeStruct((kk, t), F32),
                   jax.ShapeDtypeStruct((kk, t), I32), jax.ShapeDtypeStruct((e, 1), I32)),
        grid=(t // tm,),
        in_specs=[pl.BlockSpec((tm, d), lambda i: (i, 0)),
                  pl.BlockSpec((e, d), lambda i: (0, 0)),
                  pl.BlockSpec((e, 1), lambda i: (0, 0))],
        out_specs=(tok(), tok(), tok(), pl.BlockSpec((e, 1), lambda i: (0, 0))),
        scratch_shapes=[pltpu.VMEM((e, 1), F32)],
        compiler_params=_cparams(1),
        name="route",
    )(h2, w_router_t, router_bias.reshape(e, 1))


def _dest_kernel(eidx_ref, slot_ref, pstart_ref, o_ref):
    kk, tm = eidx_ref.shape
    eiota = lax.broadcasted_iota(I32, (N_EXPERTS, tm), 0)
    for k in range(kk):
        base = jnp.sum(jnp.where(eiota == eidx_ref[k:k + 1, :], pstart_ref[...], 0), axis=0, keepdims=True)
        o_ref[k:k + 1, :] = base + slot_ref[k:k + 1, :]


def _dest(eidx, slot, pstart, tm=2048):
    kk, t = eidx.shape
    tm = min(tm, t)
    tok = pl.BlockSpec((kk, tm), lambda i: (0, i))
    return pl.pallas_call(
        _dest_kernel,
        out_shape=jax.ShapeDtypeStruct((kk, t), I32),
        grid=(t // tm,),
        in_specs=[tok, tok, pl.BlockSpec((N_EXPERTS, 1), lambda i: (0, 0))],
        out_specs=tok,
        compiler_params=_cparams(1),
        name="dest",
    )(eidx, slot, pstart.reshape(N_EXPERTS, 1))


def _pack_halves(x):
    n = x.shape[1] // 2
    hi = lax.bitcast_convert_type(x[:, :n].astype(BF16).astype(F32), I32)
    lo = lax.bitcast_convert_type(x[:, n:].astype(BF16).astype(F32), I32)
    return hi | lax.shift_right_logical(lo, jnp.full_like(lo, 16))


def _unpack_halves(p):
    hi = lax.bitcast_convert_type(p & jnp.int32(-65536), F32)
    lo = lax.bitcast_convert_type(p << 16, F32)
    return hi, lo


def _sc_gather_rows(table, idx):
    n = idx.shape[0]
    w = table.shape[1]
    workers = SC_CORES * SC_SUBCORES
    assert n % (workers * SC_WINDOW * 2) == 0
    per_worker = n // workers
    n_win = per_worker // SC_WINDOW
    mesh = plsc.VectorSubcoreMesh(core_axis_name="c", subcore_axis_name="s")

    @functools.partial(
        pl.kernel, mesh=mesh,
        out_type=jax.ShapeDtypeStruct((n, w), table.dtype),
        scratch_types=[pltpu.VMEM((SC_WINDOW,), I32), pltpu.VMEM((SC_WINDOW,), I32),
                       pltpu.VMEM((SC_WINDOW, w), table.dtype), pltpu.VMEM((SC_WINDOW, w), table.dtype),
                       pltpu.SemaphoreType.DMA, pltpu.SemaphoreType.DMA],
    )
    def gather(table_hbm, idx_hbm, out_hbm, idx0, idx1, rows0, rows1, sem0, sem1):
        wid = lax.axis_index("s") * SC_CORES + lax.axis_index("c")

        def start(g, idx_v, rows_v, sem):
            pltpu.sync_copy(idx_hbm.at[pl.ds(wid * per_worker + g * SC_WINDOW, SC_WINDOW)], idx_v)
            pltpu.make_async_copy(table_hbm.at[idx_v], rows_v, sem).start()

        def finish(g, idx_v, rows_v, sem):
            pltpu.make_async_copy(table_hbm.at[idx_v], rows_v, sem).wait()
            pltpu.sync_copy(rows_v, out_hbm.at[pl.ds(wid * per_worker + g * SC_WINDOW, SC_WINDOW)])

        start(0, idx0, rows0, sem0)

        @pl.loop(0, n_win // 2)
        def _(h):
            g = 2 * h
            start(g + 1, idx1, rows1, sem1)
            finish(g, idx0, rows0, sem0)

            @pl.when(g + 2 < n_win)
            def _():
                start(g + 2, idx0, rows0, sem0)

            finish(g + 1, idx1, rows1, sem1)

    return gather(table, idx)


def _sc_scatter_rows(rows, idx, n_out):
    t, w = rows.shape
    kk = idx.shape[0]
    win = LANES
    workers = SC_CORES * SC_SUBCORES
    assert t % (workers * win) == 0
    per_worker = t // workers
    mesh = plsc.VectorSubcoreMesh(core_axis_name="c", subcore_axis_name="s")

    @functools.partial(
        pl.kernel, mesh=mesh,
        out_type=jax.ShapeDtypeStruct((n_out, w), rows.dtype),
        scratch_types=[pltpu.VMEM((kk, win), I32), pltpu.VMEM((win, w), rows.dtype),
                       pltpu.SemaphoreType.DMA],
    )
    def scatter(rows_hbm, idx_hbm, out_hbm, idx_v, rows_v, sem):
        wid = lax.axis_index("s") * SC_CORES + lax.axis_index("c")

        @pl.loop(0, per_worker // win)
        def _(g):
            base = wid * per_worker + g * win
            pltpu.sync_copy(idx_hbm.at[:, pl.ds(base, win)], idx_v)
            pltpu.sync_copy(rows_hbm.at[pl.ds(base, win)], rows_v)
            for k in range(kk):
                pltpu.make_async_copy(rows_v, out_hbm.at[idx_v.at[k]], sem).start()
            for k in range(kk):
                pltpu.make_async_copy(rows_v, out_hbm.at[idx_v.at[k]], sem).wait()

    return scatter(rows, idx)


def _ffn_kernel(be_ref, valid_ref, xs_ref, wg_ref, wu_ref, wd_ref, ys_ref, wgu_ref, wdn_ref):
    i = pl.program_id(0)
    ff = wg_ref.shape[-1]
    rb, half = xs_ref.shape

    @pl.when((i == 0) | (be_ref[i] != be_ref[jnp.maximum(i - 1, 0)]))
    def _():
        wgu_ref[:, :ff] = wg_ref[0].astype(BF16)
        wgu_ref[:, ff:] = wu_ref[0].astype(BF16)
        wdn_ref[...] = wd_ref[0].astype(BF16)

    @pl.when(valid_ref[i] > 0)
    def _():
        is_token = lax.broadcasted_iota(I32, (rb, half), 0) < valid_ref[i]
        x_hi, x_lo = _unpack_halves(jnp.where(is_token, xs_ref[...], jnp.zeros((rb, half), xs_ref.dtype)))
        au = _dot(jnp.concatenate([x_hi.astype(BF16), x_lo.astype(BF16)], axis=1), wgu_ref[...])
        a, u = au[:, :ff], au[:, ff:]
        act = (a * _sigmoid(a) * u).astype(BF16)
        ys_ref[...] = _pack_halves(_dot(act, wdn_ref[...]))

    @pl.when(valid_ref[i] <= 0)
    def _():
        ys_ref[...] = jnp.zeros_like(ys_ref)


def _ffn(blk_expert, blk_valid, xs, wg, wu, wd):
    n_rows, w = xs.shape
    rb = ROW_BLOCK
    _, d, ff = wg.shape
    return pl.pallas_call(
        _ffn_kernel,
        out_shape=jax.ShapeDtypeStruct((n_rows, w), I32),
        grid_spec=pltpu.PrefetchScalarGridSpec(
            num_scalar_prefetch=2,
            grid=(n_rows // rb,),
            in_specs=[pl.BlockSpec((rb, w), lambda i, be, bv: (i, 0)),
                      pl.BlockSpec((1, d, ff), lambda i, be, bv: (be[i], 0, 0)),
                      pl.BlockSpec((1, d, ff), lambda i, be, bv: (be[i], 0, 0)),
                      pl.BlockSpec((1, ff, d), lambda i, be, bv: (be[i], 0, 0))],
            out_specs=pl.BlockSpec((rb, w), lambda i, be, bv: (i, 0)),
            scratch_shapes=[pltpu.VMEM((d, 2 * ff), BF16), pltpu.VMEM((ff, d), BF16)],
        ),
        compiler_params=_cparams(1),
        name="ffn",
    )(blk_expert, blk_valid, xs, wg, wu, wd)


def _combine_kernel(rows_ref, wts_ref, h_ref, x1_ref, gate2_ref, fg_ref, wgus_ref, wds_ref, o_ref):
    half = h_ref.shape[-1]
    ff = wds_ref.shape[0]
    h_hi, h_lo = _unpack_halves(h_ref[0])
    au = _dot(jnp.concatenate([h_hi.astype(BF16), h_lo.astype(BF16)], axis=1), wgus_ref[...])
    a, u = au[:, :ff], au[:, ff:]
    y = _dot((a * _sigmoid(a) * u).astype(BF16), wds_ref[...])

    w = wts_ref[0]
    y_hi, y_lo = y[:, :half], y[:, half:]
    for k in range(TOPK_EXPERTS):
        r_hi, r_lo = _unpack_halves(rows_ref[k, 0])
        y_hi = y_hi + w[:, k:k + 1] * r_hi
        y_lo = y_lo + w[:, k:k + 1] * r_lo
    y = jnp.concatenate([y_hi, y_lo], axis=1)
    x2 = x1_ref[0] + gate2_ref[0] * y
    o_ref[0] = x2 * lax.rsqrt(jnp.mean(x2 * x2, axis=-1, keepdims=True) + EPS) * fg_ref[...]


def _combine(rows, wts, h2p, x1, gate2, final_g, wgus, wds, tb=256):
    b, s, d = x1.shape
    kk = rows.shape[0]
    w = h2p.shape[-1]
    row = lambda n: pl.BlockSpec((1, tb, n), lambda bi, i: (bi, i, 0))
    full = lambda a: pl.BlockSpec(a.shape, lambda bi, i: (0,) * a.ndim)
    return pl.pallas_call(
        _combine_kernel,
        out_shape=jax.ShapeDtypeStruct((b, s, d), F32),
        grid=(b, s // tb),
        in_specs=[pl.BlockSpec((kk, 1, tb, w), lambda bi, i: (0, bi, i, 0)),
                  row(kk), row(w), row(d),
                  pl.BlockSpec((1, 1, d), lambda bi, i: (bi, 0, 0)),
                  full(final_g), full(wgus), full(wds)],
        out_specs=row(d),
        compiler_params=_cparams(2),
        name="combine",
    )(rows, wts, h2p, x1, gate2, final_g, wgus, wds)


def _layer(x, mod, norm1_g, norm2_g, w_in, w_branch_sb, w_branch_mb, w_out, w_router, router_bias,
           w_gate_e, w_up_e, w_down_e, w_gate_sh, w_up_sh, w_down_sh, bias_tab, final_g):
    b, s, d = x.shape
    t = b * s
    nqkv = 3 * SB_WIDTH + 3 * MB_WIDTH

    col = jnp.arange(nqkv)
    is_q = (col < SB_WIDTH) | ((col >= 3 * SB_WIDTH) & (col < 3 * SB_WIDTH + MB_WIDTH))
    wqkv = (w_in[:, :nqkv] * jnp.where(is_q, HEAD_DIM ** -0.5, 1.0)).astype(BF16)
    wg = w_in[:, nqkv:].astype(BF16)
    qkv, gates = _in_proj(x, mod, norm1_g.reshape(1, d), wqkv, wg)

    cb = SB_WIDTH // LANES
    o_sb = _sb_attn(qkv, 0, cb, 2 * cb)
    o_mb = _mb_attn(qkv, bias_tab, 3 * cb, 4 * cb, 5 * cb)

    x1, h2, h2p = _out_proj(o_sb, o_mb, gates, x, mod, norm2_g.reshape(1, d), w_branch_sb.astype(BF16),
                            w_branch_mb.astype(BF16), w_out.astype(BF16))

    eidx, wts, slot, counts = _route(h2.reshape(t, d), w_router.T, router_bias)

    rb = ROW_BLOCK
    counts = counts.reshape(N_EXPERTS)
    pcounts = (counts + rb - 1) // rb * rb
    pend = jnp.cumsum(pcounts)
    pstart = pend - pcounts
    n_blk = t * TOPK_EXPERTS // rb + N_EXPERTS
    blk_row = jnp.arange(n_blk, dtype=I32) * rb
    blk_expert = jnp.minimum(jnp.sum(pend[None, :] <= blk_row[:, None], axis=1), N_EXPERTS - 1).astype(I32)
    blk_valid = jnp.clip((pstart + counts)[blk_expert] - blk_row, 0, rb).astype(I32)
    dest = _dest(eidx, slot, pstart.astype(I32))

    xs = _sc_scatter_rows(h2p.reshape(t, d // 2), dest, n_blk * rb)
    ys = _ffn(blk_expert, blk_valid, xs, w_gate_e, w_up_e, w_down_e)
    rows = _sc_gather_rows(ys, dest.reshape(TOPK_EXPERTS * t)).reshape(TOPK_EXPERTS, b, s, d // 2)
    wts_tok = wts.T.reshape(b, s, TOPK_EXPERTS)
    wgus = jnp.concatenate([w_gate_sh, w_up_sh], axis=1).astype(BF16)
    return _combine(rows, wts_tok, h2p, x1, mod[:, 5:6, :], final_g.reshape(1, d), wgus, w_down_sh.astype(BF16))


def kernel(x, c, norm1_g, norm2_g, w_ada, b_ada, w_in, w_branch_sb, w_branch_mb, w_out, w_router, router_bias,
           w_gate_e, w_up_e, w_down_e, w_gate_sh, w_up_sh, w_down_sh, rel_bias, final_g):
    b, s, d = x.shape
    depth = w_ada.shape[0]
    assert depth == 1, "the final norm is fused into the only layer's combine step"
    bias_tab = _moba_bias_table(rel_bias)
    l = 0
    mod = _ada(c, w_ada[l], b_ada[l]).reshape(b, N_MOD, d)
    return _layer(x, mod, norm1_g[l], norm2_g[l], w_in[l], w_branch_sb[l], w_branch_mb[l], w_out[l],
                  w_router[l], router_bias[l], w_gate_e[l], w_up_e[l], w_down_e[l], w_gate_sh[l],
                  w_up_sh[l], w_down_sh[l], bias_tab, final_g)
```
